```python
import jax, jax.numpy as jnp
from jax import lax
import numpy as np

D_MODEL = 1024
BATCH = 16
SEQ = 4096
DEPTH = 1
DEC_BATCH = 128
DEC_SEQ = 8
PAST_LEN = 8192
PAGE_SIZE = 128

MIX_W = D_MODEL
HEAD_DIM = 64
A_HEADS = (MIX_W // 2) // HEAD_DIM
B_HEADS = (MIX_W - A_HEADS * HEAD_DIM) // HEAD_DIM
A_W = A_HEADS * HEAD_DIM
B_W = B_HEADS * HEAD_DIM
R_DECAY = 64
R_A = 64
R_GATE = 128
SHIFT_W = 3 * B_W + R_DECAY + R_A + R_GATE
IN_W = 3 * A_W + SHIFT_W
D_FF = 11 * D_MODEL // 4
CONV_W = 3
PLE_DIM = 256
Q_BLOCK = 128
NORM_EPS = 1e-6
GN_EPS = 64e-5

kernel_name = 'stickbreak_rwkv7_hybrid_decode_step'

F32 = jnp.float32


def rmsnorm(x, g):
    xf = x.astype(F32)
    y = xf * lax.rsqrt(jnp.mean(xf * xf, axis=-1, keepdims=True) + NORM_EPS)
    return (y * g.astype(F32)).astype(x.dtype)


def stick_breaking_weights(z, visible):
    last = z.ndim - 1
    log_beta = jax.nn.log_sigmoid(z)
    log_keep = jnp.where(visible, jax.nn.log_sigmoid(-z), 0.0)
    log_rest = lax.cumsum(log_keep, axis=last, reverse=True) - log_keep
    return jnp.where(visible, jnp.exp(log_beta + log_rest), 0.0)


def sb_attend_prompt(q, k, v, bias):
    t = q.shape[1]
    scale = HEAD_DIM ** -0.5
    b = bias.astype(F32)[None, :, None, None]
    outs = []
    for blk in range(t // Q_BLOCK):
        q0 = blk * Q_BLOCK
        q1 = q0 + Q_BLOCK
        z = jnp.einsum('bthd,bshd->bhts', q[:, q0:q1].astype(F32), k[:, :q1].astype(F32)) * scale + b
        visible = jnp.arange(q1)[None, :] < jnp.arange(q0, q1)[:, None]
        a = stick_breaking_weights(z, visible)
        outs.append(jnp.einsum('bhts,bshd->bthd', a, v[:, :q1].astype(F32)))
    return jnp.concatenate(outs, axis=1).astype(q.dtype)


def sb_attend_sample(q, k, v, bias, k_past, v_past):
    t = q.shape[1]
    p = k_past.shape[1]
    scale = HEAD_DIM ** -0.5
    b = bias.astype(F32)[None, :, None, None]
    qf = q.astype(F32)
    z_past = jnp.einsum('bthd,bshd->bhts', qf, k_past.astype(F32)) * scale + b
    z_new = jnp.einsum('bthd,bshd->bhts', qf, k.astype(F32)) * scale + b
    z = jnp.concatenate([z_past, z_new], axis=-1)
    visible = jnp.concatenate([jnp.ones((t, p), bool),
                               jnp.arange(t)[None, :] < jnp.arange(t)[:, None]], axis=-1)
    a = stick_breaking_weights(z, visible)
    out = (jnp.einsum('bhts,bshd->bthd', a[..., :p], v_past.astype(F32))
           + jnp.einsum('bhts,bshd->bthd', a[..., p:], v.astype(F32)))
    return out.astype(q.dtype)


def wkv7_scan(r, w, k, v, a, b, s0):
    def step(s, inp):
        r_t, w_t, k_t, v_t, a_t, b_t = inp
        sa = jnp.einsum('bhij,bhj->bhi', s, a_t)
        s = s * w_t[:, :, None, :] + sa[..., None] * b_t[:, :, None, :] + v_t[..., None] * k_t[:, :, None, :]
        return s, jnp.einsum('bhij,bhj->bhi', s, r_t)
    xs = tuple(jnp.moveaxis(u.astype(F32), 1, 0) for u in (r, w, k, v, a, b))
    s_last, ys = lax.scan(step, s0.astype(F32), xs)
    return jnp.moveaxis(ys, 0, 1), s_last


def rwkv7_time_mix(xs, s0, lp):
    bsz, t = xs.shape[0], xs.shape[1]
    xs = xs.astype(F32)
    r = xs[..., :B_W]
    k = xs[..., B_W:2 * B_W]
    v = xs[..., 2 * B_W:3 * B_W]
    o = 3 * B_W
    xw = xs[..., o:o + R_DECAY]
    o = o + R_DECAY
    xa = xs[..., o:o + R_A]
    o = o + R_A
    xg = xs[..., o:o + R_GATE]
    w_raw = -jax.nn.softplus(-(lp['decay_base'] + jnp.tanh(xw) @ lp['w_decay_up'])) - 0.5
    decay = jnp.exp(-jnp.exp(w_raw))
    a = jax.nn.sigmoid(lp['a_base'] + xa @ lp['w_a_up'])
    g = jax.nn.sigmoid(xg) @ lp['w_g_up']
    kk = k * lp['k_k']
    k = k * (1.0 + (a - 1.0) * lp['k_a'])
    heads = lambda u: u.reshape(bsz, t, B_HEADS, HEAD_DIM)
    r, k, v, a, decay, kk = heads(r), heads(k), heads(v), heads(a), heads(decay), heads(kk)
    kk = kk / jnp.maximum(jnp.sqrt(jnp.sum(kk * kk, axis=-1, keepdims=True)), 1e-12)
    y, s_new = wkv7_scan(r, decay, k, v, -kk, kk * a, s0)
    mu = jnp.mean(y, axis=-1, keepdims=True)
    var = jnp.mean(jnp.square(y - mu), axis=-1, keepdims=True)
    y = ((y - mu) * lax.rsqrt(var + GN_EPS) * lp['lnx_w'].reshape(B_HEADS, HEAD_DIM)
         + lp['lnx_b'].reshape(B_HEADS, HEAD_DIM))
    y = y + jnp.sum(r * k * lp['r_k'], axis=-1, keepdims=True) * v
    return y.reshape(bsz, t, B_W) * g, s_new


def hybrid_layer(x, pe, attend, wkv0, shift0, conv0, lp):
    bsz, t = x.shape[0], x.shape[1]
    u = rmsnorm(x, lp['g_mix'])
    proj = u @ lp['w_in']
    q = proj[..., :A_W].reshape(bsz, t, A_HEADS, HEAD_DIM)
    k = proj[..., A_W:2 * A_W].reshape(bsz, t, A_HEADS, HEAD_DIM)
    v = proj[..., 2 * A_W:3 * A_W].reshape(bsz, t, A_HEADS, HEAD_DIM)
    att = rmsnorm(attend(q, k, v, lp['sb_bias']), lp['g_att']).reshape(bsz, t, A_W)
    xb = proj[..., 3 * A_W:]
    prev = jnp.concatenate([shift0[:, None, :].astype(xb.dtype), xb[:, :-1]], axis=1)
    xs = xb + (prev - xb) * lp['mu_shift']
    tm, wkv_new = rwkv7_time_mix(xs, wkv0, lp)
    mixed = jnp.concatenate([att, tm.astype(att.dtype)], axis=-1) @ lp['w_out']
    h = x + mixed
    un = rmsnorm(h, lp['g_ffn'])
    up = un @ lp['w_ffn_in']
    gate_pre, val = up[..., :D_FF], up[..., D_FF:]
    ext = jnp.concatenate([conv0.astype(gate_pre.dtype), gate_pre], axis=1)
    c = lp['conv_b'] + ext[:, :t] * lp['conv_w'][0]
    for j in range(1, CONV_W):
        c = c + ext[:, j:j + t] * lp['conv_w'][j]
    h = h + (jax.nn.silu(c) * val) @ lp['w_ffn_out']
    pe_proj = rmsnorm(pe @ lp['w_ple_in'], lp['g_ple'])
    h = h + jax.nn.sigmoid(h @ lp['w_ple_gate']) * pe_proj
    return h, k, v, wkv_new, xb[:, -1], ext[:, ext.shape[1] - (CONV_W - 1):]


def setup_inputs(seed: int = 0) -> dict:
    key = jax.random.key(seed)
    ks = iter(jax.random.split(key, 48))
    nrm = lambda shape, s: jax.random.normal(next(ks), shape, F32) * s
    gain = lambda shape: 1.0 + jax.random.normal(next(ks), shape, F32) * 0.02
    n_pages = PAST_LEN // PAGE_SIZE
    n_used = DEC_BATCH * n_pages
    n_phys = n_used + max(1, n_used // 4)
    x_prompt = nrm((BATCH, SEQ, D_MODEL), 1.0)
    x_sample = nrm((DEC_BATCH, DEC_SEQ, D_MODEL), 1.0)
    cache_k = nrm((DEPTH, n_phys, PAGE_SIZE, A_HEADS, HEAD_DIM), 1.0)
    cache_v = nrm((DEPTH, n_phys, PAGE_SIZE, A_HEADS, HEAD_DIM), 1.0)
    state_wkv = nrm((DEPTH, DEC_BATCH, B_HEADS, HEAD_DIM, HEAD_DIM), 0.3)
    state_shift = nrm((DEPTH, DEC_BATCH, SHIFT_W), 1.0)
    state_conv = nrm((DEPTH, DEC_BATCH, CONV_W - 1, D_FF), 1.0)
    page_table = jax.random.permutation(next(ks), n_phys)[:n_used].reshape(DEC_BATCH, n_pages).astype(jnp.int32)
    p_prompt = nrm((DEPTH, BATCH, SEQ, PLE_DIM), 1.0)
    p_sample = nrm((DEPTH, DEC_BATCH, DEC_SEQ, PLE_DIM), 1.0)
    sb_bias = jnp.linspace(-9.0, -5.0, A_HEADS, dtype=F32)[None, :] + nrm((DEPTH, A_HEADS), 0.1)
    return {
        'x_prompt': x_prompt,
        'x_sample': x_sample,
        'cache_k': cache_k,
        'cache_v': cache_v,
        'state_wkv': state_wkv,
        'state_shift': state_shift,
        'state_conv': state_conv,
        'page_table': page_table,
        'p_prompt': p_prompt,
        'p_sample': p_sample,
        'g_mix': gain((DEPTH, D_MODEL)),
        'w_in': nrm((DEPTH, D_MODEL, IN_W), D_MODEL ** -0.5),
        'mu_shift': jax.random.uniform(next(ks), (DEPTH, SHIFT_W), F32),
        'w_decay_up': nrm((DEPTH, R_DECAY, B_W), 0.5 * R_DECAY ** -0.5),
        'decay_base': jax.random.uniform(next(ks), (DEPTH, B_W), F32, -3.0, 0.5),
        'w_a_up': nrm((DEPTH, R_A, B_W), 0.5 * R_A ** -0.5),
        'a_base': nrm((DEPTH, B_W), 0.1),
        'w_g_up': nrm((DEPTH, R_GATE, B_W), R_GATE ** -0.5),
        'k_k': 0.85 + nrm((DEPTH, B_W), 0.02),
        'k_a': gain((DEPTH, B_W)),
        'r_k': nrm((DEPTH, B_HEADS, HEAD_DIM), 0.1),
        'lnx_w': gain((DEPTH, B_W)),
        'lnx_b': nrm((DEPTH, B_W), 0.02),
        'g_att': gain((DEPTH, A_HEADS, HEAD_DIM)),
        'sb_bias': sb_bias,
        'w_out': nrm((DEPTH, MIX_W, D_MODEL), MIX_W ** -0.5),
        'g_ffn': gain((DEPTH, D_MODEL)),
        'w_ffn_in': nrm((DEPTH, D_MODEL, 2 * D_FF), D_MODEL ** -0.5),
        'conv_w': nrm((DEPTH, CONV_W, D_FF), CONV_W ** -0.5),
        'conv_b': nrm((DEPTH, D_FF), 0.02),
        'w_ffn_out': nrm((DEPTH, D_FF, D_MODEL), D_FF ** -0.5),
        'w_ple_in': nrm((DEPTH, PLE_DIM, D_MODEL), PLE_DIM ** -0.5),
        'g_ple': gain((DEPTH, D_MODEL)),
        'w_ple_gate': nrm((DEPTH, D_MODEL, D_MODEL), D_MODEL ** -0.5),
        'g_final': gain((D_MODEL,)),
    }


def reference(x_prompt, x_sample, cache_k, cache_v, state_wkv, state_shift, state_conv, page_table,
              p_prompt, p_sample, g_mix, w_in, mu_shift, w_decay_up, decay_base, w_a_up, a_base,
              w_g_up, k_k, k_a, r_k, lnx_w, lnx_b, g_att, sb_bias, w_out, g_ffn, w_ffn_in, conv_w,
              conv_b, w_ffn_out, w_ple_in, g_ple, w_ple_gate, g_final):
    bp = x_prompt.shape[0]
    bs = x_sample.shape[0]
    n_past = page_table.shape[1] * cache_k.shape[2]
    hp = x_prompt
    hs = x_sample
    kp_l, vp_l, wp_l, sp_l, cp_l = [], [], [], [], []
    ks_l, vs_l, ws_l, ss_l, cs_l = [], [], [], [], []
    for i in range(DEPTH):
        lp = {
            'g_mix': g_mix[i], 'w_in': w_in[i], 'mu_shift': mu_shift[i],
            'w_decay_up': w_decay_up[i], 'decay_base': decay_base[i],
            'w_a_up': w_a_up[i], 'a_base': a_base[i], 'w_g_up': w_g_up[i],
            'k_k': k_k[i], 'k_a': k_a[i], 'r_k': r_k[i], 'lnx_w': lnx_w[i], 'lnx_b': lnx_b[i],
            'g_att': g_att[i], 'sb_bias': sb_bias[i], 'w_out': w_out[i], 'g_ffn': g_ffn[i],
            'w_ffn_in': w_ffn_in[i], 'conv_w': conv_w[i], 'conv_b': conv_b[i],
            'w_ffn_out': w_ffn_out[i], 'w_ple_in': w_ple_in[i], 'g_ple': g_ple[i],
            'w_ple_gate': w_ple_gate[i],
        }
        hp, kp, vp, wp, sp, cp = hybrid_layer(
            hp, p_prompt[i], sb_attend_prompt,
            jnp.zeros((bp, B_HEADS, HEAD_DIM, HEAD_DIM), F32),
            jnp.zeros((bp, SHIFT_W), x_prompt.dtype),
            jnp.zeros((bp, CONV_W - 1, D_FF), x_prompt.dtype), lp)
        k_past = cache_k[i][page_table].reshape(bs, n_past, A_HEADS, HEAD_DIM)
        v_past = cache_v[i][page_table].reshape(bs, n_past, A_HEADS, HEAD_DIM)

        def attend_sample(q, k, v, bias, k_past=k_past, v_past=v_past):
            return sb_attend_sample(q, k, v, bias, k_past, v_past)

        hs, ksm, vsm, wsm, ssm, csm = hybrid_layer(
            hs, p_sample[i], attend_sample, state_wkv[i], state_shift[i], state_conv[i], lp)
        kp_l.append(kp.astype(cache_k.dtype))
        vp_l.append(vp.astype(cache_v.dtype))
        wp_l.append(wp.astype(state_wkv.dtype))
        sp_l.append(sp.astype(state_shift.dtype))
        cp_l.append(cp.astype(state_conv.dtype))
        ks_l.append(ksm.astype(cache_k.dtype))
        vs_l.append(vsm.astype(cache_v.dtype))
        ws_l.append(wsm.astype(state_wkv.dtype))
        ss_l.append(ssm.astype(state_shift.dtype))
        cs_l.append(csm.astype(state_conv.dtype))
    y_prompt = rmsnorm(hp, g_final).astype(x_prompt.dtype)
    y_sample = rmsnorm(hs, g_final).astype(x_sample.dtype)
    k_prompt = jnp.stack(kp_l)
    v_prompt = jnp.stack(vp_l)
    wkv_prompt = jnp.stack(wp_l)
    shift_prompt = jnp.stack(sp_l)
    conv_prompt = jnp.stack(cp_l)
    k_sample = jnp.stack(ks_l)
    v_sample = jnp.stack(vs_l)
    wkv_sample = jnp.stack(ws_l)
    shift_sample = jnp.stack(ss_l)
    conv_sample = jnp.stack(cs_l)
    return (y_prompt, y_sample, k_prompt, v_prompt, wkv_prompt, shift_prompt, conv_prompt,
            k_sample, v_sample, wkv_sample, shift_sample, conv_sample)
```

```python
import functools

import jax
import jax.numpy as jnp
from jax import lax
from jax.experimental import pallas as pl
from jax.experimental.pallas import tpu as pltpu

F32 = jnp.float32
BF16 = jnp.bfloat16
HIGHEST = lax.Precision.HIGHEST
NORM_EPS = 1e-6
GN_EPS = 64e-5
KK_EPS = 1e-12
CONV_W = 3
V7X_VMEM_LIMIT_BYTES = 56 * 1024 * 1024
LANES = 128
SUBLANES = 8


def _cparams(*semantics):
    return pltpu.CompilerParams(dimension_semantics=semantics, vmem_limit_bytes=V7X_VMEM_LIMIT_BYTES)


def _const_spec(shape):
    return pl.BlockSpec(shape, lambda *_: (0,) * len(shape), pipeline_mode=pl.Buffered(1))


def _tile(n, pref):
    t = min(n, pref)
    assert n % t == 0, (n, pref)
    return t


def _rms(x, g):
    return x * lax.rsqrt(jnp.mean(x * x, axis=-1, keepdims=True) + NORM_EPS) * g


def _mm(a, b):
    return jnp.dot(a.astype(BF16), b.astype(BF16), preferred_element_type=F32)


def _mm_hi(a, b, dims=(((1,), (0,)), ((), ()))):
    return lax.dot_general(a, b, dims, precision=HIGHEST, preferred_element_type=F32)


def _split(x):
    hi = x.astype(BF16)
    return hi, (x - hi.astype(F32)).astype(BF16)


def _mm_split_l(x, c):
    hi, lo = _split(x)
    return jnp.dot(hi, c, preferred_element_type=F32) + jnp.dot(lo, c, preferred_element_type=F32)


def _mm_split_r(c, x):
    hi, lo = _split(x)
    return jnp.dot(c, hi, preferred_element_type=F32) + jnp.dot(c, lo, preferred_element_type=F32)


def _softplus_tail(z):
    return jnp.log1p(jnp.exp(-jnp.abs(z)))


def _proj_in_kernel(x_ref, g_ref, wq_ref, wk_ref, wv_ref, wx_ref, xb_ref, *outs, heads, head_major):
    rows = x_ref.shape[0]
    a_w = wq_ref.shape[1]
    hd = a_w // heads
    u = _rms(x_ref[...], g_ref[...]).astype(BF16)
    q = jnp.dot(u, wq_ref[...], preferred_element_type=F32)
    xb_ref[...] = jnp.dot(u, wx_ref[...], preferred_element_type=F32)
    if head_major:
        qh_ref, kt_ref, vt_ref, ktb_ref, vtb_ref = outs
        nt = (((1,), (1,)), ((), ()))
        kt = lax.dot_general(wk_ref[...], u, nt, preferred_element_type=F32).reshape(heads, hd, rows)
        vt = lax.dot_general(wv_ref[...], u, nt, preferred_element_type=F32).reshape(heads, hd, rows)
        kt_ref[...] = kt
        vt_ref[...] = vt
        ktb_ref[...] = kt.astype(BF16)
        vtb_ref[...] = vt.astype(BF16)
        qs = q * (hd ** -0.5)
        for h in range(heads):
            qh_ref[h] = qs[:, h * hd:(h + 1) * hd].astype(BF16)
    else:
        q_ref, k_ref, v_ref = outs
        q_ref[...] = q
        k_ref[...] = jnp.dot(u, wk_ref[...], preferred_element_type=F32)
        v_ref[...] = jnp.dot(u, wv_ref[...], preferred_element_type=F32)


def _proj_in(x2, g, w_in, *, a_w, heads, seq_rows, head_major):
    n, d = x2.shape
    shift_w = w_in.shape[1] - 3 * a_w
    hd = a_w // heads
    tm = _tile(seq_rows if head_major else n, 512)
    row = lambda i: (i, 0)
    wq, wk, wv, wx = (w_in[:, :a_w], w_in[:, a_w:2 * a_w], w_in[:, 2 * a_w:3 * a_w], w_in[:, 3 * a_w:])
    out_shape = [jax.ShapeDtypeStruct((n, shift_w), F32)]
    out_specs = [pl.BlockSpec((tm, shift_w), row)]
    if head_major:
        bsz, per_seq = n // seq_rows, seq_rows // tm
        wk, wv = wk.T, wv.T
        kv_w_spec = _const_spec((a_w, d))
        t_spec = pl.BlockSpec((None, heads, hd, tm), lambda i: (i // per_seq, 0, 0, i % per_seq))
        out_shape += [jax.ShapeDtypeStruct((bsz, heads, seq_rows, hd), BF16)]
        out_specs += [pl.BlockSpec((None, heads, tm, hd), lambda i: (i // per_seq, 0, i % per_seq, 0))]
        out_shape += [jax.ShapeDtypeStruct((bsz, heads, hd, seq_rows), dt) for dt in (F32, F32, BF16, BF16)]
        out_specs += [t_spec] * 4
    else:
        kv_w_spec = _const_spec((d, a_w))
        out_shape += [jax.ShapeDtypeStruct((n, a_w), F32)] * 3
        out_specs += [pl.BlockSpec((tm, a_w), row)] * 3
    return pl.pallas_call(
        functools.partial(_proj_in_kernel, heads=heads, head_major=head_major),
        grid=(n // tm,),
        in_specs=[pl.BlockSpec((tm, d), row), _const_spec((1, d)), _const_spec((d, a_w)), kv_w_spec, kv_w_spec,
                  _const_spec((d, shift_w))],
        out_specs=out_specs,
        out_shape=out_shape,
        compiler_params=_cparams("parallel"),
        name="proj_in",
    )(x2, g, wq.astype(BF16), wk.astype(BF16), wv.astype(BF16), wx.astype(BF16))


def _sb_scores(z):
    t = _softplus_tail(z)
    return jnp.minimum(z, 0.0) - t, jnp.minimum(-z, 0.0) - t


def _sb_prompt_kernel(bias_ref, q_ref, kt_ref, vt_ref, g_ref, o_ref, oh_ref, *, tq, heads):
    qi = pl.program_id(1)
    hd = q_ref.shape[-1]
    row = lax.broadcasted_iota(jnp.int32, (tq, tq), 0)
    col = lax.broadcasted_iota(jnp.int32, (tq, tq), 1)
    vis = col < row
    later = (row > col).astype(BF16)
    nt = (((1,), (1,)), ((), ()))

    def tile(q, kt, vt, b, carry, acc, diagonal):
        z = jnp.dot(q, kt, preferred_element_type=F32) + b
        log_beta, log_keep = _sb_scores(z)
        if diagonal:
            log_keep = jnp.where(vis, log_keep, 0.0)
        rest = jnp.dot(log_keep.astype(BF16), later, preferred_element_type=F32) + carry
        a = jnp.exp(log_beta + rest)
        if diagonal:
            a = jnp.where(vis, a, 0.0)
        acc = acc + lax.dot_general(a.astype(BF16), vt, nt, preferred_element_type=F32)
        carry = carry + jnp.sum(log_keep, axis=-1, keepdims=True)
        return carry, acc

    def head(h, _):
        q = q_ref[h]
        b = bias_ref[h]
        d0 = pl.multiple_of(qi * tq, tq)
        carry, acc = tile(q, kt_ref[h, :, pl.ds(d0, tq)], vt_ref[h, :, pl.ds(d0, tq)], b,
                          jnp.zeros((tq, 1), F32), jnp.zeros((tq, hd), F32), True)

        def earlier(j, c):
            s0 = pl.multiple_of((qi - 1 - j) * tq, tq)
            return tile(q, kt_ref[h, :, pl.ds(s0, tq)], vt_ref[h, :, pl.ds(s0, tq)], b, c[0], c[1], False)

        carry, acc = lax.fori_loop(0, qi, earlier, (carry, acc))
        oh_ref[h] = _rms(acc, g_ref[pl.ds(h, 1), :])
        return 0

    lax.fori_loop(0, heads, head, 0)
    o_ref[...] = jnp.concatenate([oh_ref[h] for h in range(heads)], axis=-1).astype(o_ref.dtype)


def _sb_prompt(qh, ktb, vtb, bias, g_att):
    bsz, heads, t, hd = qh.shape
    tq = _tile(t, LANES)
    kv_spec = pl.BlockSpec((None, heads, hd, t), lambda b, i: (b, 0, 0, 0))
    return pl.pallas_call(
        functools.partial(_sb_prompt_kernel, tq=tq, heads=heads),
        grid=(bsz, t // tq),
        in_specs=[pl.BlockSpec(memory_space=pltpu.SMEM),
                  pl.BlockSpec((None, heads, tq, hd), lambda b, i: (b, 0, i, 0)),
                  kv_spec, kv_spec, _const_spec((heads, hd))],
        out_specs=pl.BlockSpec((tq, heads * hd), lambda b, i: (b * (t // tq) + i, 0)),
        out_shape=jax.ShapeDtypeStruct((bsz * t, heads * hd), BF16),
        scratch_shapes=[pltpu.VMEM((heads, tq, hd), F32)],
        compiler_params=_cparams("parallel", "arbitrary"),
        name="sb_prompt",
    )(bias, qh, ktb, vtb, g_att)


def _sb_sample_kernel(pt_ref, bias_ref, q_ref, kn_ref, vn_ref, g_ref, *rest, heads, pp):
    del pt_ref
    k_refs, v_refs = rest[:pp], rest[pp:2 * pp]
    o_ref = rest[2 * pp]
    qh_ref, acc_ref, carry_ref = rest[2 * pp + 1:]
    p = pl.program_id(1)
    t_new, a_w = q_ref.shape
    hd = a_w // heads
    page = k_refs[0].shape[-1]
    nt = (((1,), (1,)), ((), ()))
    hrows = lambda x, h: x[h * t_new:(h + 1) * t_new]

    def later_mask(m):
        return (lax.broadcasted_iota(jnp.int32, (m, m), 0) > lax.broadcasted_iota(jnp.int32, (m, m), 1)).astype(BF16)

    @pl.when(p == 0)
    def _():
        q = q_ref[...] * (hd ** -0.5)
        kn = kn_ref[...].astype(BF16)
        vn = vn_ref[...].astype(BF16)
        for h in range(heads):
            qh_ref[h] = q[:, h * hd:(h + 1) * hd].astype(BF16)
        z = jnp.concatenate(
            [lax.dot_general(qh_ref[h], kn[:, h * hd:(h + 1) * hd], nt, preferred_element_type=F32) + bias_ref[h]
             for h in range(heads)], axis=0)
        vis1 = (lax.broadcasted_iota(jnp.int32, (t_new, t_new), 1)
                < lax.broadcasted_iota(jnp.int32, (t_new, t_new), 0))
        vis = jnp.concatenate([vis1] * heads, axis=0)
        log_beta, log_keep = _sb_scores(z)
        log_keep = jnp.where(vis, log_keep, 0.0)
        rest_ = jnp.dot(log_keep.astype(BF16), later_mask(t_new), preferred_element_type=F32)
        a = jnp.where(vis, jnp.exp(log_beta + rest_), 0.0).astype(BF16)
        for h in range(heads):
            acc_ref[h] = jnp.dot(hrows(a, h), vn[:, h * hd:(h + 1) * hd], preferred_element_type=F32)
        carry_ref[...] = jnp.sum(log_keep, axis=-1, keepdims=True)

    later = later_mask(page)
    carry = carry_ref[...]
    accs = [acc_ref[h] for h in range(heads)]
    for j in range(pp):
        z = jnp.concatenate(
            [jnp.dot(qh_ref[h], k_refs[j][h].astype(BF16), preferred_element_type=F32) + bias_ref[h]
             for h in range(heads)], axis=0)
        log_beta, log_keep = _sb_scores(z)
        rest_ = jnp.dot(log_keep.astype(BF16), later, preferred_element_type=F32) + carry
        a = jnp.exp(log_beta + rest_).astype(BF16)
        for h in range(heads):
            accs[h] = accs[h] + lax.dot_general(hrows(a, h), v_refs[j][h].astype(BF16), nt,
                                                preferred_element_type=F32)
        carry = carry + jnp.sum(log_keep, axis=-1, keepdims=True)
    carry_ref[...] = carry
    for h in range(heads):
        acc_ref[h] = accs[h]

    @pl.when(p == pl.num_programs(1) - 1)
    def _():
        o_ref[...] = jnp.concatenate([_rms(accs[h], g_ref[pl.ds(h, 1), :]) for h in range(heads)],
                                     axis=-1).astype(o_ref.dtype)


def _sb_sample(q2, k2, v2, cache_kt, cache_vt, page_table, bias, g_att, *, t_new):
    n, a_w = q2.shape
    heads, hd = g_att.shape
    bsz = n // t_new
    page = cache_kt.shape[-1]
    n_pages = page_table.shape[1]
    pp = _tile(n_pages, 8)

    def page_spec(j):
        return pl.BlockSpec((None, heads, hd, page), lambda b, p, pt: (pt[b, n_pages - 1 - p * pp - j], 0, 0, 0))

    tok = pl.BlockSpec((t_new, a_w), lambda b, p, pt: (b, 0))
    grid_spec = pltpu.PrefetchScalarGridSpec(
        num_scalar_prefetch=1,
        grid=(bsz, n_pages // pp),
        in_specs=[pl.BlockSpec(memory_space=pltpu.SMEM), tok, tok, tok,
                  pl.BlockSpec((heads, hd), lambda b, p, pt: (0, 0))] + [page_spec(j) for j in range(pp)] * 2,
        out_specs=tok,
        scratch_shapes=[pltpu.VMEM((heads, t_new, hd), BF16), pltpu.VMEM((heads, t_new, hd), F32),
                        pltpu.VMEM((heads * t_new, 1), F32)],
    )
    return pl.pallas_call(
        functools.partial(_sb_sample_kernel, heads=heads, pp=pp),
        grid_spec=grid_spec,
        out_shape=jax.ShapeDtypeStruct((n, a_w), BF16),
        compiler_params=_cparams("parallel", "arbitrary"),
        name="sb_sample",
    )(page_table, bias, q2, k2, v2, g_att, *([cache_kt] * pp), *([cache_vt] * pp))


def _rwkv_kernel(xb_ref, shift0_ref, wkv0_ref, mu_ref, wd_ref, wa_ref, wg_ref, dbase_ref, abase_ref,
                 kk_ref, ka_ref, rk_ref, lnw_ref, lnb_ref, gsum_ref, ltri_ref,
                 tm_ref, wkv_ref, s_ref, prev_ref, y_ref, *, heads):
    c = pl.program_id(1)
    rows, _ = xb_ref.shape
    b_w = tm_ref.shape[-1]
    hd = b_w // heads

    @pl.when(c == 0)
    def _():
        s_ref[...] = wkv0_ref[...]
        prev_ref[...] = shift0_ref[...]

    xb = xb_ref[...]
    rowi = lax.broadcasted_iota(jnp.int32, (rows, 1), 0)
    prev = jnp.where(rowi == 0, prev_ref[...], pltpu.roll(xb, 1, axis=0))
    prev_ref[...] = xb[rows - 1:rows, :]
    xs = xb + (prev - xb) * mu_ref[...]
    r = xs[:, 0:b_w]
    k = xs[:, b_w:2 * b_w]
    v = xs[:, 2 * b_w:3 * b_w]
    tail = xs[:, 3 * b_w:]

    dec_in = dbase_ref[...] + _mm(jnp.tanh(tail), wd_ref[...])
    w_raw = -(jnp.maximum(-dec_in, 0.0) + _softplus_tail(dec_in)) - 0.5
    logw = -jnp.exp(w_raw)
    a_lr = jax.nn.sigmoid(abase_ref[...] + _mm(tail, wa_ref[...]))
    gate = _mm(jax.nn.sigmoid(tail), wg_ref[...])
    kk = k * kk_ref[...]
    k2 = k * (1.0 + (a_lr - 1.0) * ka_ref[...])
    kk = kk / jnp.maximum(jnp.sqrt(_mm_split_l(kk * kk, gsum_ref[...])), KK_EPS)

    cum = _mm_split_r(ltri_ref[...], logw)
    e_pos = jnp.exp(cum)
    e_neg = jnp.exp(-cum)
    a_hat = -kk * jnp.exp(cum - logw)
    b_hat = kk * a_lr * e_neg
    k_hat = k2 * e_neg
    r_hat = r * e_pos
    g_end = e_pos[rows - 1:rows, :]

    ri = lax.broadcasted_iota(jnp.int32, (rows, rows), 0)
    ci = lax.broadcasted_iota(jnp.int32, (rows, rows), 1)
    strict = ci < ri
    incl = ci <= ri
    eye = (ci == ri).astype(F32)
    nt = (((1,), (1,)), ((), ()))
    tn = (((0,), (0,)), ((), ()))
    doublings = max(1, (rows - 1).bit_length())

    for h in range(heads):
        sl = slice(h * hd, (h + 1) * hd)
        ah, bh, kh, rh, vh = a_hat[:, sl], b_hat[:, sl], k_hat[:, sl], r_hat[:, sl], v[:, sl]
        s0 = s_ref[h]
        m = _mm_hi(jnp.concatenate([ah, rh], axis=0), jnp.concatenate([bh, kh], axis=0), nt)
        a_ab = jnp.where(strict, m[0:rows, 0:rows], 0.0)
        a_ak = jnp.where(strict, m[0:rows, rows:], 0.0)
        a_rb = jnp.where(incl, m[rows:, 0:rows], 0.0)
        a_rk = jnp.where(incl, m[rows:, rows:], 0.0)
        inv = eye + a_ab
        pw = a_ab
        for _ in range(doublings - 1):
            pw = _mm_hi(pw, pw)
            inv = inv + _mm_hi(inv, pw)
        u = _mm_hi(inv, _mm_hi(ah, s0, nt) + _mm_hi(a_ak, vh))
        y_ref[h] = _mm_hi(rh, s0, nt) + _mm_hi(a_rb, u) + _mm_hi(a_rk, vh)
        s_ref[h] = (s0 + _mm_hi(u, bh, tn) + _mm_hi(vh, kh, tn)) * g_end[:, sl]

    y = jnp.concatenate([y_ref[h] for h in range(heads)], axis=-1)
    gmean = gsum_ref[...]
    mean = _mm_split_l(y, gmean) * (1.0 / hd)
    d = y - mean
    var = _mm_split_l(d * d, gmean) * (1.0 / hd)
    yn = d * lax.rsqrt(var + GN_EPS) * lnw_ref[...] + lnb_ref[...]
    bonus = _mm_split_l(r * k2 * rk_ref[...], gmean) * v
    tm_ref[...] = ((yn + bonus) * gate).astype(tm_ref.dtype)

    @pl.when(c == pl.num_programs(1) - 1)
    def _():
        wkv_ref[...] = s_ref[...]


def _rwkv(xb2, shift0, wkv0, lp, *, seq_rows):
    n, shift_w = xb2.shape
    bsz, heads, hd, _ = wkv0.shape
    b_w = heads * hd
    tail_w = shift_w - 3 * b_w
    rows = _tile(seq_rows, 64)
    per_seq = seq_rows // rows
    r_dec, r_a, r_gate = lp['w_decay_up'].shape[0], lp['w_a_up'].shape[0], lp['w_g_up'].shape[0]
    assert r_dec + r_a + r_gate == tail_w

    def pad_rows(w, start):
        return jnp.zeros((tail_w, b_w), BF16).at[start:start + w.shape[0]].set(w.astype(BF16))

    wd = pad_rows(lp['w_decay_up'], 0)
    wa = pad_rows(lp['w_a_up'], r_dec)
    wg = pad_rows(lp['w_g_up'], r_dec + r_a)
    ch = jnp.arange(b_w)
    gsum = (ch[:, None] // hd == ch[None, :] // hd).astype(BF16)
    ltri = (jnp.arange(rows)[None, :] <= jnp.arange(rows)[:, None]).astype(BF16)
    vec = lambda a: a.reshape(1, -1).astype(F32)
    state_spec = pl.BlockSpec((None, heads, hd, hd), lambda b, c: (b, 0, 0, 0))
    vb = _const_spec((1, b_w))
    return pl.pallas_call(
        functools.partial(_rwkv_kernel, heads=heads),
        grid=(bsz, per_seq),
        in_specs=[pl.BlockSpec((rows, shift_w), lambda b, c: (b * per_seq + c, 0)),
                  pl.BlockSpec((None, 1, shift_w), lambda b, c: (b, 0, 0)),
                  state_spec, _const_spec((1, shift_w)),
                  _const_spec((tail_w, b_w)), _const_spec((tail_w, b_w)), _const_spec((tail_w, b_w)),
                  vb, vb, vb, vb, vb, vb, vb, _const_spec((b_w, b_w)), _const_spec((rows, rows))],
        out_specs=[pl.BlockSpec((rows, b_w), lambda b, c: (b * per_seq + c, 0)), state_spec],
        out_shape=[jax.ShapeDtypeStruct((n, b_w), BF16), jax.ShapeDtypeStruct(wkv0.shape, F32)],
        scratch_shapes=[pltpu.VMEM((heads, hd, hd), F32), pltpu.VMEM((1, shift_w), F32),
                        pltpu.VMEM((heads, rows, hd), F32)],
        compiler_params=_cparams("parallel", "arbitrary"),
        name="rwkv",
    )(xb2, shift0.reshape(bsz, 1, shift_w), wkv0, vec(lp['mu_shift']), wd, wa, wg, vec(lp['decay_base']),
      vec(lp['a_base']), vec(lp['k_k']), vec(lp['k_a']), vec(lp['r_k']), vec(lp['lnx_w']), vec(lp['lnx_b']),
      gsum, ltri)


def _ffn_kernel(x_ref, att_ref, tmix_ref, pe_ref, st0_ref, st1_ref, woa_ref, wob_ref, gffn_ref, wg_ref, wv_ref,
                cw_ref, cb_ref, wfo_ref, wpi_ref, gple_ref, wpg_ref, gfin_ref, y_ref, tail_ref,
                c0_ref, c1_ref, *, seq_rows, final_norm):
    rows = x_ref.shape[0]
    carried = seq_rows >= rows
    h = (x_ref[...] + jnp.dot(att_ref[...], woa_ref[...], preferred_element_type=F32)
         + jnp.dot(tmix_ref[...], wob_ref[...], preferred_element_type=F32))
    un = _rms(h, gffn_ref[...]).astype(BF16)
    gp = jnp.dot(un, wg_ref[...], preferred_element_type=F32)
    val = jnp.dot(un, wv_ref[...], preferred_element_type=F32)

    rowi = lax.broadcasted_iota(jnp.int32, (rows, 1), 0)
    if carried:
        tpos = rowi
        first = pl.program_id(0) % (seq_rows // rows) == 0

        @pl.when(first)
        def _():
            c0_ref[...] = st0_ref[...]
            c1_ref[...] = st1_ref[...]

        h0, h1 = c0_ref[...], c1_ref[...]
    else:
        assert seq_rows & (seq_rows - 1) == 0
        tpos = jnp.bitwise_and(rowi, seq_rows - 1)
        h0, h1 = st0_ref[...], st1_ref[...]
    prev1 = jnp.where(tpos == 0, h1, pltpu.roll(gp, 1, axis=0))
    prev2 = jnp.where(tpos == 0, h0, jnp.where(tpos == 1, h1, pltpu.roll(gp, 2, axis=0)))
    cw = cw_ref[...]
    cpre = cb_ref[...] + prev2 * cw[0:1, :] + prev1 * cw[1:2, :] + gp * cw[2:3, :]
    act = cpre * jax.nn.sigmoid(cpre) * val
    if carried:
        c0_ref[...] = gp[rows - 2:rows - 1, :]
        c1_ref[...] = gp[rows - 1:rows, :]
    tail_rows = tail_ref.shape[0]
    tail_ref[...] = gp[rows - tail_rows:rows, :]

    h = h + _mm(act, wfo_ref[...])
    pe_proj = _rms(_mm(pe_ref[...], wpi_ref[...]), gple_ref[...])
    h = h + jax.nn.sigmoid(_mm(h, wpg_ref[...])) * pe_proj
    y_ref[...] = _rms(h, gfin_ref[...]) if final_norm else h


def _ffn(x2, att, tmix, pe2, st0, st1, lp, g_final, *, seq_rows, final_norm):
    n, d = x2.shape
    a_w = att.shape[1]
    d_ff = lp['conv_w'].shape[1]
    ple = pe2.shape[1]
    assert lp['conv_w'].shape[0] == CONV_W
    bf = lambda a: a.astype(BF16)
    vec = lambda a: a.reshape(1, -1).astype(F32)
    row = lambda i: (i, 0)
    if seq_rows >= 256:
        tm = _tile(seq_rows, 256)
        per_seq = seq_rows // tm
        assert tm >= SUBLANES and seq_rows >= CONV_W - 1
        st_spec = pl.BlockSpec((None, 1, d_ff), lambda i: (i // per_seq, 0, 0))
        tail_rows = SUBLANES
        tail_shape = jax.ShapeDtypeStruct((n // seq_rows, tail_rows, d_ff), F32)
        tail_spec = pl.BlockSpec((None, tail_rows, d_ff), lambda i: (i // per_seq, 0, 0))
        semantics = "arbitrary"
    else:
        tm = _tile(n, 256)
        assert tm % seq_rows == 0
        st_spec = pl.BlockSpec((tm, d_ff), row)
        tail_rows = tm
        tail_shape = jax.ShapeDtypeStruct((n, d_ff), F32)
        tail_spec = pl.BlockSpec((tm, d_ff), row)
        semantics = "parallel"
    w_in = lp['w_ffn_in']
    return pl.pallas_call(
        functools.partial(_ffn_kernel, seq_rows=seq_rows, final_norm=final_norm),
        grid=(n // tm,),
        in_specs=[pl.BlockSpec((tm, d), row), pl.BlockSpec((tm, a_w), row), pl.BlockSpec((tm, tmix.shape[1]), row),
                  pl.BlockSpec((tm, ple), row), st_spec, st_spec,
                  _const_spec((a_w, d)), _const_spec((tmix.shape[1], d)), _const_spec((1, d)),
                  _const_spec((d, d_ff)), _const_spec((d, d_ff)), _const_spec((CONV_W, d_ff)), _const_spec((1, d_ff)),
                  _const_spec((d_ff, d)), _const_spec((ple, d)), _const_spec((1, d)), _const_spec((d, d)),
                  _const_spec((1, d))],
        out_specs=[pl.BlockSpec((tm, d), row), tail_spec],
        out_shape=[jax.ShapeDtypeStruct((n, d), F32), tail_shape],
        scratch_shapes=[pltpu.VMEM((1, d_ff), F32), pltpu.VMEM((1, d_ff), F32)],
        compiler_params=_cparams(semantics),
        name="ffn",
    )(x2, att, tmix, pe2, st0, st1, bf(lp['w_out'][:a_w]), bf(lp['w_out'][a_w:]), vec(lp['g_ffn']),
      bf(w_in[:, :d_ff]), bf(w_in[:, d_ff:]), lp['conv_w'].astype(F32), vec(lp['conv_b']), bf(lp['w_ffn_out']),
      bf(lp['w_ple_in']), vec(lp['g_ple']), bf(lp['w_ple_gate']), vec(g_final))


def _layer(x, pe, wkv0, shift0, conv0, lp, g_final, final_norm, past):
    bsz, t, d = x.shape
    heads, hd = lp['g_att'].shape
    a_w = heads * hd
    n = bsz * t
    x2 = x.reshape(n, d)
    prompt = past is None
    bias, g_att = lp['sb_bias'].astype(F32), lp['g_att'].astype(F32)
    xb2, *proj = _proj_in(x2, lp['g_mix'].reshape(1, d).astype(F32), lp['w_in'], a_w=a_w, heads=heads,
                          seq_rows=t, head_major=prompt)
    if prompt:
        qh, kt, vt, ktb, vtb = proj
        att = _sb_prompt(qh, ktb, vtb, bias, g_att)
        k_new, v_new = jnp.transpose(kt, (0, 3, 1, 2)), jnp.transpose(vt, (0, 3, 1, 2))
    else:
        q2, k2, v2 = proj
        cache_k, cache_v, page_table = past
        cache_kt, cache_vt = jnp.transpose(cache_k, (0, 2, 3, 1)), jnp.transpose(cache_v, (0, 2, 3, 1))
        att = _sb_sample(q2, k2, v2, cache_kt, cache_vt, page_table, bias, g_att, t_new=t)
        k_new, v_new = k2.reshape(bsz, t, heads, hd), v2.reshape(bsz, t, heads, hd)
    tmix, wkv_new = _rwkv(xb2, shift0, wkv0, lp, seq_rows=t)
    d_ff = conv0.shape[-1]
    if t >= 256:
        st0, st1 = conv0[:, 0:1, :], conv0[:, 1:2, :]
    else:
        st0 = jnp.repeat(conv0[:, 0, :], t, axis=0)
        st1 = jnp.repeat(conv0[:, 1, :], t, axis=0)
    y2, tail = _ffn(x2, att, tmix, pe.reshape(n, -1), st0, st1, lp, g_final, seq_rows=t, final_norm=final_norm)
    conv_new = tail.reshape(bsz, -1, d_ff)[:, -(CONV_W - 1):, :]
    return (y2.reshape(bsz, t, d), k_new, v_new, wkv_new, xb2.reshape(bsz, t, -1)[:, -1, :], conv_new)


def kernel(x_prompt, x_sample, cache_k, cache_v, state_wkv, state_shift, state_conv, page_table, p_prompt, p_sample, g_mix, w_in, mu_shift, w_decay_up, decay_base, w_a_up, a_base, w_g_up, k_k, k_a, r_k, lnx_w, lnx_b, g_att, sb_bias, w_out, g_ffn, w_ffn_in, conv_w, conv_b, w_ffn_out, w_ple_in, g_ple, w_ple_gate, g_final):
    depth = g_mix.shape[0]
    bp = x_prompt.shape[0]
    heads_b, hd = r_k.shape[1], r_k.shape[2]
    hp, hs = x_prompt, x_sample
    outs_p, outs_s = [], []
    for i in range(depth):
        lp = dict(g_mix=g_mix[i], w_in=w_in[i], mu_shift=mu_shift[i], w_decay_up=w_decay_up[i],
                  decay_base=decay_base[i], w_a_up=w_a_up[i], a_base=a_base[i], w_g_up=w_g_up[i], k_k=k_k[i],
                  k_a=k_a[i], r_k=r_k[i], lnx_w=lnx_w[i], lnx_b=lnx_b[i], g_att=g_att[i], sb_bias=sb_bias[i],
                  w_out=w_out[i], g_ffn=g_ffn[i], w_ffn_in=w_ffn_in[i], conv_w=conv_w[i], conv_b=conv_b[i],
                  w_ffn_out=w_ffn_out[i], w_ple_in=w_ple_in[i], g_ple=g_ple[i], w_ple_gate=w_ple_gate[i])
        last = i == depth - 1
        hp, *rest_p = _layer(hp, p_prompt[i], jnp.zeros((bp, heads_b, hd, hd), F32),
                             jnp.zeros((bp, state_shift.shape[-1]), F32),
                             jnp.zeros((bp,) + state_conv.shape[2:], F32), lp, g_final, last, None)
        hs, *rest_s = _layer(hs, p_sample[i], state_wkv[i], state_shift[i], state_conv[i], lp, g_final, last,
                             (cache_k[i], cache_v[i], page_table))
        outs_p.append(rest_p)
        outs_s.append(rest_s)
    stack = lambda outs, j: jnp.stack([o[j] for o in outs])
    return (hp, hs) + tuple(stack(outs_p, j) for j in range(5)) + tuple(stack(outs_s, j) for j in range(5))
```

```python
import functools

import jax
import jax.numpy as jnp
from jax import lax
from jax.experimental import pallas as pl
from jax.experimental.pallas import tpu as pltpu

F32 = jnp.float32
BF16 = jnp.bfloat16
NORM_EPS = 1e-6
GN_EPS = 64e-5
KK_EPS = 1e-12
CONV_W = 3
V7X_VMEM_LIMIT_BYTES = 56 * 1024 * 1024
LANES = 128
SUBLANES = 8
LOG2E = 1.4426950408889634


def _cparams(*semantics):
    return pltpu.CompilerParams(dimension_semantics=semantics, vmem_limit_bytes=V7X_VMEM_LIMIT_BYTES)


def _const_spec(shape):
    return pl.BlockSpec(shape, lambda *_: (0,) * len(shape), pipeline_mode=pl.Buffered(1))


def _tile(n, pref):
    t = min(n, pref)
    assert n % t == 0, (n, pref)
    return t


def _rms(x, g):
    return x * lax.rsqrt(jnp.mean(x * x, axis=-1, keepdims=True) + NORM_EPS) * g


NN = (((1,), (0,)), ((), ()))
NT = (((1,), (1,)), ((), ()))
TN = (((0,), (0,)), ((), ()))


def _mm(a, b, dims=NN):
    return lax.dot_general(a.astype(BF16), b.astype(BF16), dims, preferred_element_type=F32)


def _split(x):
    hi = x.astype(BF16)
    return hi, (x - hi.astype(F32)).astype(BF16)


def _mm_split_l(x, c):
    hi, lo = _split(x)
    return jnp.dot(hi, c, preferred_element_type=F32) + jnp.dot(lo, c, preferred_element_type=F32)


def _mm_split_r(c, x):
    hi, lo = _split(x)
    return jnp.dot(c, hi, preferred_element_type=F32) + jnp.dot(c, lo, preferred_element_type=F32)


def _softplus_tail(z):
    return jnp.log(1.0 + jnp.exp(-jnp.abs(z)))


def _proj_in_kernel(x_ref, g_ref, wq_ref, wk_ref, wv_ref, wx_ref, xb_ref, *outs, heads, head_major):
    rows = x_ref.shape[0]
    a_w = wq_ref.shape[1]
    hd = a_w // heads
    u = _rms(x_ref[...], g_ref[...]).astype(BF16)
    q = jnp.dot(u, wq_ref[...], preferred_element_type=F32)
    xb_ref[...] = jnp.dot(u, wx_ref[...], preferred_element_type=F32)
    if head_major:
        qh_ref, kt_ref, vt_ref, ktb_ref, vtb_ref = outs
        kt = _mm(wk_ref[...], u, NT).reshape(heads, hd, rows)
        vt = _mm(wv_ref[...], u, NT).reshape(heads, hd, rows)
        kt_ref[...] = kt
        vt_ref[...] = vt
        ktb_ref[...] = kt.astype(BF16)
        vtb_ref[...] = vt.astype(BF16)
        qs = q * (hd ** -0.5 * LOG2E)
        for h in range(heads):
            qh_ref[h] = qs[:, h * hd:(h + 1) * hd].astype(BF16)
    else:
        q_ref, k_ref, v_ref = outs
        q_ref[...] = q
        k_ref[...] = jnp.dot(u, wk_ref[...], preferred_element_type=F32)
        v_ref[...] = jnp.dot(u, wv_ref[...], preferred_element_type=F32)


def _proj_in(x2, g, w_in, *, a_w, heads, seq_rows, head_major):
    n, d = x2.shape
    shift_w = w_in.shape[1] - 3 * a_w
    hd = a_w // heads
    tm = _tile(seq_rows if head_major else n, 512)
    row = lambda i: (i, 0)
    wq, wk, wv, wx = (w_in[:, :a_w], w_in[:, a_w:2 * a_w], w_in[:, 2 * a_w:3 * a_w], w_in[:, 3 * a_w:])
    out_shape = [jax.ShapeDtypeStruct((n, shift_w), F32)]
    out_specs = [pl.BlockSpec((tm, shift_w), row)]
    if head_major:
        bsz, per_seq = n // seq_rows, seq_rows // tm
        wk, wv = wk.T, wv.T
        kv_w_spec = _const_spec((a_w, d))
        t_spec = pl.BlockSpec((None, heads, hd, tm), lambda i: (i // per_seq, 0, 0, i % per_seq))
        out_shape += [jax.ShapeDtypeStruct((bsz, heads, seq_rows, hd), BF16)]
        out_specs += [pl.BlockSpec((None, heads, tm, hd), lambda i: (i // per_seq, 0, i % per_seq, 0))]
        out_shape += [jax.ShapeDtypeStruct((bsz, heads, hd, seq_rows), dt) for dt in (F32, F32, BF16, BF16)]
        out_specs += [t_spec] * 4
    else:
        kv_w_spec = _const_spec((d, a_w))
        out_shape += [jax.ShapeDtypeStruct((n, a_w), F32)] * 3
        out_specs += [pl.BlockSpec((tm, a_w), row)] * 3
    return pl.pallas_call(
        functools.partial(_proj_in_kernel, heads=heads, head_major=head_major),
        grid=(n // tm,),
        in_specs=[pl.BlockSpec((tm, d), row), _const_spec((1, d)), _const_spec((d, a_w)), kv_w_spec, kv_w_spec,
                  _const_spec((d, shift_w))],
        out_specs=out_specs,
        out_shape=out_shape,
        compiler_params=_cparams("parallel"),
        name="proj_in",
    )(x2, g, wq.astype(BF16), wk.astype(BF16), wv.astype(BF16), wx.astype(BF16))


def _sb_scores(z):
    log_beta = jnp.minimum(z, 0.0) - _softplus_tail(z)
    return log_beta, log_beta - z


def _sb_scores2(z2):
    log_beta = jnp.minimum(z2, 0.0) - jnp.log2(1.0 + jnp.exp2(-jnp.abs(z2)))
    return log_beta, log_beta - z2


def _sb_prompt_kernel(bias_ref, q_ref, kt_ref, vt_ref, g_ref, o_ref, acc_ref, carry_ref, *, tq, heads):
    qi = pl.program_id(1)
    row = lax.broadcasted_iota(jnp.int32, (tq, tq), 0)
    col = lax.broadcasted_iota(jnp.int32, (tq, tq), 1)
    vis = col < row
    later = jnp.concatenate([row > col, row >= 0], axis=1).astype(BF16)
    hs = range(heads)

    def visit(s0, diagonal):
        z = [jnp.dot(q_ref[h], kt_ref[h, :, pl.ds(s0, tq)], preferred_element_type=F32) + bias_ref[h] for h in hs]
        scores = [_sb_scores2(z[h]) for h in hs]
        log_beta = [s[0] for s in scores]
        log_keep = [jnp.where(vis, s[1], 0.0) if diagonal else s[1] for s in scores]
        sums = [jnp.dot(log_keep[h].astype(BF16), later, preferred_element_type=F32) for h in hs]
        if diagonal:
            rest = [sums[h][:, :tq] for h in hs]
        else:
            rest = [sums[h][:, :tq] + carry_ref[h] for h in hs]
        a = [jnp.exp2(log_beta[h] + rest[h]) for h in hs]
        if diagonal:
            a = [jnp.where(vis, x, 0.0) for x in a]
        pv = [_mm(a[h], vt_ref[h, :, pl.ds(s0, tq)], NT) for h in hs]
        for h in hs:
            if diagonal:
                acc_ref[h] = pv[h]
                carry_ref[h] = sums[h][:, tq:]
            else:
                acc_ref[h] += pv[h]
                carry_ref[h] += sums[h][:, tq:]

    visit(pl.multiple_of(qi * tq, tq), True)

    def earlier(j, _):
        visit(pl.multiple_of((qi - 1 - j) * tq, tq), False)
        return 0

    lax.fori_loop(0, qi, earlier, 0)
    o_ref[...] = jnp.concatenate([_rms(acc_ref[h], g_ref[h:h + 1, :]) for h in hs], axis=-1).astype(o_ref.dtype)


def _sb_prompt(qh, ktb, vtb, bias, g_att):
    bsz, heads, t, hd = qh.shape
    tq = _tile(t, 2 * LANES)
    kv_spec = pl.BlockSpec((None, heads, hd, t), lambda b, i: (b, 0, 0, 0))
    return pl.pallas_call(
        functools.partial(_sb_prompt_kernel, tq=tq, heads=heads),
        grid=(bsz, t // tq),
        in_specs=[pl.BlockSpec(memory_space=pltpu.SMEM),
                  pl.BlockSpec((None, heads, tq, hd), lambda b, i: (b, 0, i, 0)),
                  kv_spec, kv_spec, _const_spec((heads, hd))],
        out_specs=pl.BlockSpec((tq, heads * hd), lambda b, i: (b * (t // tq) + i, 0)),
        out_shape=jax.ShapeDtypeStruct((bsz * t, heads * hd), BF16),
        scratch_shapes=[pltpu.VMEM((heads, tq, hd), F32), pltpu.VMEM((heads, tq, tq), F32)],
        compiler_params=_cparams("parallel", "arbitrary"),
        name="sb_prompt",
    )(bias, qh, ktb, vtb, g_att)


def _sb_sample_kernel(pt_ref, bias_ref, q_ref, kn_ref, vn_ref, g_ref, *rest, heads, pp):
    del pt_ref
    k_refs, v_refs = rest[:pp], rest[pp:2 * pp]
    o_ref = rest[2 * pp]
    qh_ref, acc_ref, carry_ref = rest[2 * pp + 1:]
    p = pl.program_id(1)
    t_new, a_w = q_ref.shape
    hd = a_w // heads
    page = k_refs[0].shape[-1]
    hrows = lambda x, h: x[h * t_new:(h + 1) * t_new]

    def later_mask(m):
        return (lax.broadcasted_iota(jnp.int32, (m, m), 0) > lax.broadcasted_iota(jnp.int32, (m, m), 1)).astype(BF16)

    @pl.when(p == 0)
    def _():
        q = q_ref[...] * (hd ** -0.5)
        kn = kn_ref[...].astype(BF16)
        vn = vn_ref[...].astype(BF16)
        for h in range(heads):
            qh_ref[h] = q[:, h * hd:(h + 1) * hd].astype(BF16)
        z = jnp.concatenate(
            [lax.dot_general(qh_ref[h], kn[:, h * hd:(h + 1) * hd], NT, preferred_element_type=F32) + bias_ref[h]
             for h in range(heads)], axis=0)
        vis1 = (lax.broadcasted_iota(jnp.int32, (t_new, t_new), 1)
                < lax.broadcasted_iota(jnp.int32, (t_new, t_new), 0))
        vis = jnp.concatenate([vis1] * heads, axis=0)
        log_beta, log_keep = _sb_scores(z)
        log_keep = jnp.where(vis, log_keep, 0.0)
        rest_ = jnp.dot(log_keep.astype(BF16), later_mask(t_new), preferred_element_type=F32)
        a = jnp.where(vis, jnp.exp(log_beta + rest_), 0.0).astype(BF16)
        for h in range(heads):
            acc_ref[h] = jnp.dot(hrows(a, h), vn[:, h * hd:(h + 1) * hd], preferred_element_type=F32)
        carry_ref[...] = jnp.broadcast_to(jnp.sum(log_keep, axis=-1, keepdims=True), carry_ref.shape)

    ones = lax.broadcasted_iota(jnp.int32, (page, page), 0) >= 0
    later = jnp.concatenate([later_mask(page), ones.astype(BF16)], axis=1)
    hs, js = range(heads), range(pp)
    z = [jnp.concatenate([jnp.dot(qh_ref[h], k_refs[j][h].astype(BF16), preferred_element_type=F32) + bias_ref[h]
                          for h in hs], axis=0) for j in js]
    scores = [_sb_scores(z[j]) for j in js]
    sums = [jnp.dot(scores[j][1].astype(BF16), later, preferred_element_type=F32) for j in js]
    carry = carry_ref[...]
    a = []
    for j in js:
        a.append(jnp.exp(scores[j][0] + sums[j][:, :page] + carry).astype(BF16))
        carry = carry + sums[j][:, page:]
    carry_ref[...] = carry
    for h in hs:
        acc = acc_ref[h]
        for j in js:
            acc = acc + lax.dot_general(hrows(a[j], h), v_refs[j][h].astype(BF16), NT, preferred_element_type=F32)
        acc_ref[h] = acc

    @pl.when(p == pl.num_programs(1) - 1)
    def _():
        o_ref[...] = jnp.concatenate([_rms(acc_ref[h], g_ref[h:h + 1, :]) for h in hs], axis=-1).astype(o_ref.dtype)


def _sb_sample(q2, k2, v2, cache_kt, cache_vt, page_table, bias, g_att, *, t_new):
    n, a_w = q2.shape
    heads, hd = g_att.shape
    bsz = n // t_new
    page = cache_kt.shape[-1]
    n_pages = page_table.shape[1]
    pp = _tile(n_pages, 8)

    def page_spec(j):
        return pl.BlockSpec((None, heads, hd, page), lambda b, p, pt: (pt[b, n_pages - 1 - p * pp - j], 0, 0, 0))

    tok = pl.BlockSpec((t_new, a_w), lambda b, p, pt: (b, 0))
    grid_spec = pltpu.PrefetchScalarGridSpec(
        num_scalar_prefetch=1,
        grid=(bsz, n_pages // pp),
        in_specs=[pl.BlockSpec(memory_space=pltpu.SMEM), tok, tok, tok,
                  pl.BlockSpec((heads, hd), lambda b, p, pt: (0, 0))] + [page_spec(j) for j in range(pp)] * 2,
        out_specs=tok,
        scratch_shapes=[pltpu.VMEM((heads, t_new, hd), BF16), pltpu.VMEM((heads, t_new, hd), F32),
                        pltpu.VMEM((heads * t_new, page), F32)],
    )
    return pl.pallas_call(
        functools.partial(_sb_sample_kernel, heads=heads, pp=pp),
        grid_spec=grid_spec,
        out_shape=jax.ShapeDtypeStruct((n, a_w), BF16),
        compiler_params=_cparams("parallel", "arbitrary"),
        name="sb_sample",
    )(page_table, bias, q2, k2, v2, g_att, *([cache_kt] * pp), *([cache_vt] * pp))


def _rwkv_kernel(xb_ref, shift0_ref, wkv0_ref, mu_ref, wd_ref, wa_ref, wg_ref, dbase_ref, abase_ref,
                 kk_ref, ka_ref, rk_ref, lnw_ref, lnb_ref, gsum_ref, ltri_ref,
                 tm_ref, wkv_ref, s_ref, prev_ref, *, heads):
    c = pl.program_id(1)
    nseq, rows, shift_w = xb_ref.shape
    n = nseq * rows
    b_w = tm_ref.shape[-1]
    hd = b_w // heads

    @pl.when(c == 0)
    def _():
        s_ref[...] = wkv0_ref[...]
        prev_ref[...] = shift0_ref[...]

    xb = xb_ref[...].reshape(n, shift_w)
    rowi = lax.broadcasted_iota(jnp.int32, (n, 1), 0)
    prev = pltpu.roll(xb, 1, axis=0)
    for s in range(nseq):
        prev = jnp.where(rowi == s * rows, prev_ref[s], prev)
        prev_ref[s] = xb[(s + 1) * rows - 1:(s + 1) * rows, :]
    xs = xb + (prev - xb) * mu_ref[...]
    r = xs[:, 0:b_w]
    k = xs[:, b_w:2 * b_w]
    v = xs[:, 2 * b_w:3 * b_w]
    tail = xs[:, 3 * b_w:]

    dec_in = dbase_ref[...] + _mm(jnp.tanh(tail), wd_ref[...])
    w_raw = -(jnp.maximum(-dec_in, 0.0) + _softplus_tail(dec_in)) - 0.5
    logw = -jnp.exp(w_raw)
    a_lr = jax.nn.sigmoid(abase_ref[...] + _mm(tail, wa_ref[...]))
    gate = _mm(jax.nn.sigmoid(tail), wg_ref[...])
    kk = k * kk_ref[...]
    k2 = k * (1.0 + (a_lr - 1.0) * ka_ref[...])
    kk = kk / jnp.maximum(jnp.sqrt(_mm_split_l(kk * kk, gsum_ref[...])), KK_EPS)

    cum = _mm_split_r(ltri_ref[...], logw)
    e_pos = jnp.exp(cum)
    e_neg = jnp.exp(-cum)
    a_hat = -kk * jnp.exp(cum - logw)
    b_hat = kk * a_lr * e_neg
    k_hat = k2 * e_neg
    r_hat = r * e_pos

    ri = lax.broadcasted_iota(jnp.int32, (rows, rows), 0)
    ci = lax.broadcasted_iota(jnp.int32, (rows, rows), 1)
    strict = ci < ri
    incl = ci <= ri
    eye = (ci == ri).astype(F32)
    doublings = max(1, (rows - 1).bit_length())

    ch = [(s, h) for s in range(nseq) for h in range(heads)]
    cut = lambda x, s, h: x[s * rows:(s + 1) * rows, h * hd:(h + 1) * hd]
    s0 = {(s, h): s_ref[s, h] for s, h in ch}
    s0b = {i: s0[i].astype(BF16) for i in ch}
    a_b, b_b, k_b, r_b, v_b = (x.astype(BF16) for x in (a_hat, b_hat, k_hat, r_hat, v))
    ah = {i: cut(a_b, *i) for i in ch}
    bh = {i: cut(b_b, *i) for i in ch}
    kh = {i: cut(k_b, *i) for i in ch}
    rh = {i: cut(r_b, *i) for i in ch}
    vh = {i: cut(v_b, *i) for i in ch}
    m = {i: _mm(jnp.concatenate([ah[i], rh[i]], axis=0), jnp.concatenate([bh[i], kh[i]], axis=0), NT) for i in ch}
    a_ab = {i: jnp.where(strict, m[i][0:rows, 0:rows], 0.0) for i in ch}
    a_ak = {i: jnp.where(strict, m[i][0:rows, rows:], 0.0) for i in ch}
    a_rb = {i: jnp.where(incl, m[i][rows:, 0:rows], 0.0) for i in ch}
    a_rk = {i: jnp.where(incl, m[i][rows:, rows:], 0.0) for i in ch}
    u_rhs = {i: _mm(ah[i], s0b[i], NT) + _mm(a_ak[i], vh[i]) for i in ch}
    y_part = {i: _mm(rh[i], s0b[i], NT) + _mm(a_rk[i], vh[i]) for i in ch}
    s_part = {i: s0[i] + _mm(vh[i], kh[i], TN) for i in ch}
    inv = {i: eye + a_ab[i] for i in ch}
    pw = a_ab
    for _ in range(doublings - 1):
        pw = {i: _mm(pw[i], pw[i]) for i in ch}
        inv = {i: inv[i] + _mm(inv[i], pw[i]) for i in ch}
    u = {i: _mm(inv[i], u_rhs[i]) for i in ch}
    ys = {i: y_part[i] + _mm(a_rb[i], u[i]) for i in ch}
    for s, h in ch:
        g_end = e_pos[(s + 1) * rows - 1:(s + 1) * rows, h * hd:(h + 1) * hd]
        s_ref[s, h] = (s_part[s, h] + _mm(u[s, h], bh[s, h], TN)) * g_end

    y = jnp.concatenate([jnp.concatenate([ys[s, h] for h in range(heads)], axis=-1) for s in range(nseq)], axis=0)
    gmean = gsum_ref[...]
    mean = _mm_split_l(y, gmean) * (1.0 / hd)
    d = y - mean
    var = _mm_split_l(d * d, gmean) * (1.0 / hd)
    yn = d * lax.rsqrt(var + GN_EPS) * lnw_ref[...] + lnb_ref[...]
    bonus = _mm_split_l(r * k2 * rk_ref[...], gmean) * v
    tm_ref[...] = ((yn + bonus) * gate).astype(tm_ref.dtype).reshape(nseq, rows, b_w)

    @pl.when(c == pl.num_programs(1) - 1)
    def _():
        wkv_ref[...] = s_ref[...]


def _rwkv(xb2, shift0, wkv0, lp, *, seq_rows):
    n, shift_w = xb2.shape
    bsz, heads, hd, _ = wkv0.shape
    b_w = heads * hd
    tail_w = shift_w - 3 * b_w
    rows = _tile(seq_rows, 64)
    per_seq = seq_rows // rows
    nseq = _tile(bsz, 4)
    r_dec, r_a, r_gate = lp['w_decay_up'].shape[0], lp['w_a_up'].shape[0], lp['w_g_up'].shape[0]
    assert r_dec + r_a + r_gate == tail_w

    def pad_rows(w, start):
        return jnp.zeros((tail_w, b_w), BF16).at[start:start + w.shape[0]].set(w.astype(BF16))

    wd = pad_rows(lp['w_decay_up'], 0)
    wa = pad_rows(lp['w_a_up'], r_dec)
    wg = pad_rows(lp['w_g_up'], r_dec + r_a)
    ch = jnp.arange(b_w)
    gsum = (ch[:, None] // hd == ch[None, :] // hd).astype(BF16)
    pos = jnp.arange(nseq * rows)
    ltri = ((pos[None, :] <= pos[:, None]) & (pos[None, :] // rows == pos[:, None] // rows)).astype(BF16)
    vec = lambda a: a.reshape(1, -1).astype(F32)
    state_spec = pl.BlockSpec((nseq, heads, hd, hd), lambda b, c: (b, 0, 0, 0))
    vb = _const_spec((1, b_w))
    tmix, wkv_new = pl.pallas_call(
        functools.partial(_rwkv_kernel, heads=heads),
        grid=(bsz // nseq, per_seq),
        in_specs=[pl.BlockSpec((nseq, rows, shift_w), lambda b, c: (b, c, 0)),
                  pl.BlockSpec((nseq, 1, shift_w), lambda b, c: (b, 0, 0)),
                  state_spec, _const_spec((1, shift_w)),
                  _const_spec((tail_w, b_w)), _const_spec((tail_w, b_w)), _const_spec((tail_w, b_w)),
                  vb, vb, vb, vb, vb, vb, vb, _const_spec((b_w, b_w)), _const_spec((nseq * rows, nseq * rows))],
        out_specs=[pl.BlockSpec((nseq, rows, b_w), lambda b, c: (b, c, 0)), state_spec],
        out_shape=[jax.ShapeDtypeStruct((bsz, seq_rows, b_w), BF16), jax.ShapeDtypeStruct(wkv0.shape, F32)],
        scratch_shapes=[pltpu.VMEM((nseq, heads, hd, hd), F32), pltpu.VMEM((nseq, 1, shift_w), F32)],
        compiler_params=_cparams("parallel", "arbitrary"),
        name="rwkv",
    )(xb2.reshape(bsz, seq_rows, shift_w), shift0.reshape(bsz, 1, shift_w), wkv0, vec(lp['mu_shift']), wd, wa, wg,
      vec(lp['decay_base']), vec(lp['a_base']), vec(lp['k_k']), vec(lp['k_a']), vec(lp['r_k']), vec(lp['lnx_w']),
      vec(lp['lnx_b']), gsum, ltri)
    return tmix.reshape(n, b_w), wkv_new


def _ffn_kernel(x_ref, att_ref, tmix_ref, pe_ref, st0_ref, st1_ref, woa_ref, wob_ref, gffn_ref, wg_ref, wv_ref,
                cw_ref, cb_ref, wfo_ref, wpi_ref, gple_ref, wpg_ref, gfin_ref, y_ref, tail_ref,
                c0_ref, c1_ref, *, seq_rows, final_norm):
    rows = x_ref.shape[0]
    carried = seq_rows >= rows
    h = (x_ref[...] + jnp.dot(att_ref[...], woa_ref[...], preferred_element_type=F32)
         + jnp.dot(tmix_ref[...], wob_ref[...], preferred_element_type=F32))
    un = _rms(h, gffn_ref[...]).astype(BF16)
    gp = jnp.dot(un, wg_ref[...], preferred_element_type=F32)
    val = jnp.dot(un, wv_ref[...], preferred_element_type=F32)

    rowi = lax.broadcasted_iota(jnp.int32, (rows, 1), 0)
    if carried:
        tpos = rowi
        first = pl.program_id(0) % (seq_rows // rows) == 0

        @pl.when(first)
        def _():
            c0_ref[...] = st0_ref[...]
            c1_ref[...] = st1_ref[...]

        h0, h1 = c0_ref[...], c1_ref[...]
    else:
        assert seq_rows & (seq_rows - 1) == 0
        tpos = jnp.bitwise_and(rowi, seq_rows - 1)
        h0, h1 = st0_ref[...], st1_ref[...]
    prev1 = jnp.where(tpos == 0, h1, pltpu.roll(gp, 1, axis=0))
    prev2 = jnp.where(tpos == 0, h0, jnp.where(tpos == 1, h1, pltpu.roll(gp, 2, axis=0)))
    cw = cw_ref[...]
    cpre = cb_ref[...] + prev2 * cw[0:1, :] + prev1 * cw[1:2, :] + gp * cw[2:3, :]
    act = cpre * jax.nn.sigmoid(cpre) * val
    if carried:
        c0_ref[...] = gp[rows - 2:rows - 1, :]
        c1_ref[...] = gp[rows - 1:rows, :]
    tail_rows = tail_ref.shape[0]
    tail_ref[...] = gp[rows - tail_rows:rows, :]

    h = h + _mm(act, wfo_ref[...])
    pe_proj = _rms(_mm(pe_ref[...], wpi_ref[...]), gple_ref[...])
    h = h + jax.nn.sigmoid(_mm(h, wpg_ref[...])) * pe_proj
    y_ref[...] = _rms(h, gfin_ref[...]) if final_norm else h


def _ffn(x2, att, tmix, pe2, st0, st1, lp, g_final, *, seq_rows, final_norm):
    n, d = x2.shape
    a_w = att.shape[1]
    d_ff = lp['conv_w'].shape[1]
    ple = pe2.shape[1]
    assert lp['conv_w'].shape[0] == CONV_W
    bf = lambda a: a.astype(BF16)
    vec = lambda a: a.reshape(1, -1).astype(F32)
    row = lambda i: (i, 0)
    if seq_rows >= 256:
        tm = _tile(seq_rows, 256)
        per_seq = seq_rows // tm
        assert tm >= SUBLANES and seq_rows >= CONV_W - 1
        st_spec = pl.BlockSpec((None, 1, d_ff), lambda i: (i // per_seq, 0, 0))
        tail_rows = SUBLANES
        tail_shape = jax.ShapeDtypeStruct((n // seq_rows, tail_rows, d_ff), F32)
        tail_spec = pl.BlockSpec((None, tail_rows, d_ff), lambda i: (i // per_seq, 0, 0))
        semantics = "arbitrary"
    else:
        tm = _tile(n, 256)
        assert tm % seq_rows == 0
        st_spec = pl.BlockSpec((tm, d_ff), row)
        tail_rows = tm
        tail_shape = jax.ShapeDtypeStruct((n, d_ff), F32)
        tail_spec = pl.BlockSpec((tm, d_ff), row)
        semantics = "parallel"
    w_in = lp['w_ffn_in']
    return pl.pallas_call(
        functools.partial(_ffn_kernel, seq_rows=seq_rows, final_norm=final_norm),
        grid=(n // tm,),
        in_specs=[pl.BlockSpec((tm, d), row), pl.BlockSpec((tm, a_w), row), pl.BlockSpec((tm, tmix.shape[1]), row),
                  pl.BlockSpec((tm, ple), row), st_spec, st_spec,
                  _const_spec((a_w, d)), _const_spec((tmix.shape[1], d)), _const_spec((1, d)),
                  _const_spec((d, d_ff)), _const_spec((d, d_ff)), _const_spec((CONV_W, d_ff)), _const_spec((1, d_ff)),
                  _const_spec((d_ff, d)), _const_spec((ple, d)), _const_spec((1, d)), _const_spec((d, d)),
                  _const_spec((1, d))],
        out_specs=[pl.BlockSpec((tm, d), row), tail_spec],
        out_shape=[jax.ShapeDtypeStruct((n, d), F32), tail_shape],
        scratch_shapes=[pltpu.VMEM((1, d_ff), F32), pltpu.VMEM((1, d_ff), F32)],
        compiler_params=_cparams(semantics),
        name="ffn",
    )(x2, att, tmix, pe2, st0, st1, bf(lp['w_out'][:a_w]), bf(lp['w_out'][a_w:]), vec(lp['g_ffn']),
      bf(w_in[:, :d_ff]), bf(w_in[:, d_ff:]), lp['conv_w'].astype(F32), vec(lp['conv_b']), bf(lp['w_ffn_out']),
      bf(lp['w_ple_in']), vec(lp['g_ple']), bf(lp['w_ple_gate']), vec(g_final))


def _layer(x, pe, wkv0, shift0, conv0, lp, g_final, final_norm, past):
    bsz, t, d = x.shape
    heads, hd = lp['g_att'].shape
    a_w = heads * hd
    n = bsz * t
    x2 = x.reshape(n, d)
    prompt = past is None
    bias, g_att = lp['sb_bias'].astype(F32), lp['g_att'].astype(F32)
    xb2, *proj = _proj_in(x2, lp['g_mix'].reshape(1, d).astype(F32), lp['w_in'], a_w=a_w, heads=heads,
                          seq_rows=t, head_major=prompt)
    if prompt:
        qh, kt, vt, ktb, vtb = proj
        att = _sb_prompt(qh, ktb, vtb, bias * LOG2E, g_att)
        k_new, v_new = jnp.transpose(kt, (0, 3, 1, 2)), jnp.transpose(vt, (0, 3, 1, 2))
    else:
        q2, k2, v2 = proj
        cache_k, cache_v, page_table = past
        cache_kt, cache_vt = jnp.transpose(cache_k, (0, 2, 3, 1)), jnp.transpose(cache_v, (0, 2, 3, 1))
        att = _sb_sample(q2, k2, v2, cache_kt, cache_vt, page_table, bias, g_att, t_new=t)
        k_new, v_new = k2.reshape(bsz, t, heads, hd), v2.reshape(bsz, t, heads, hd)
    tmix, wkv_new = _rwkv(xb2, shift0, wkv0, lp, seq_rows=t)
    d_ff = conv0.shape[-1]
    if t >= 256:
        st0, st1 = conv0[:, 0:1, :], conv0[:, 1:2, :]
    else:
        st0 = jnp.repeat(conv0[:, 0, :], t, axis=0)
        st1 = jnp.repeat(conv0[:, 1, :], t, axis=0)
    y2, tail = _ffn(x2, att, tmix, pe.reshape(n, -1), st0, st1, lp, g_final, seq_rows=t, final_norm=final_norm)
    conv_new = tail.reshape(bsz, -1, d_ff)[:, -(CONV_W - 1):, :]
    return (y2.reshape(bsz, t, d), k_new, v_new, wkv_new, xb2.reshape(bsz, t, -1)[:, -1, :], conv_new)


def kernel(x_prompt, x_sample, cache_k, cache_v, state_wkv, state_shift, state_conv, page_table, p_prompt, p_sample, g_mix, w_in, mu_shift, w_decay_up, decay_base, w_a_up, a_base, w_g_up, k_k, k_a, r_k, lnx_w, lnx_b, g_att, sb_bias, w_out, g_ffn, w_ffn_in, conv_w, conv_b, w_ffn_out, w_ple_in, g_ple, w_ple_gate, g_final):
    depth = g_mix.shape[0]
    bp = x_prompt.shape[0]
    heads_b, hd = r_k.shape[1], r_k.shape[2]
    hp, hs = x_prompt, x_sample
    outs_p, outs_s = [], []
    for i in range(depth):
        lp = dict(g_mix=g_mix[i], w_in=w_in[i], mu_shift=mu_shift[i], w_decay_up=w_decay_up[i],
                  decay_base=decay_base[i], w_a_up=w_a_up[i], a_base=a_base[i], w_g_up=w_g_up[i], k_k=k_k[i],
                  k_a=k_a[i], r_k=r_k[i], lnx_w=lnx_w[i], lnx_b=lnx_b[i], g_att=g_att[i], sb_bias=sb_bias[i],
                  w_out=w_out[i], g_ffn=g_ffn[i], w_ffn_in=w_ffn_in[i], conv_w=conv_w[i], conv_b=conv_b[i],
                  w_ffn_out=w_ffn_out[i], w_ple_in=w_ple_in[i], g_ple=g_ple[i], w_ple_gate=w_ple_gate[i])
        last = i == depth - 1
        hp, *rest_p = _layer(hp, p_prompt[i], jnp.zeros((bp, heads_b, hd, hd), F32),
                             jnp.zeros((bp, state_shift.shape[-1]), F32),
                             jnp.zeros((bp,) + state_conv.shape[2:], F32), lp, g_final, last, None)
        hs, *rest_s = _layer(hs, p_sample[i], state_wkv[i], state_shift[i], state_conv[i], lp, g_final, last,
                             (cache_k[i], cache_v[i], page_table))
        outs_p.append(rest_p)
        outs_s.append(rest_s)
    stack = lambda outs, j: jnp.stack([o[j] for o in outs])
    return (hp, hs) + tuple(stack(outs_p, j) for j in range(5)) + tuple(stack(outs_s, j) for j in range(5))
```

```python
import functools

import jax
import jax.numpy as jnp
from jax import lax
from jax.experimental import pallas as pl
from jax.experimental.pallas import tpu as pltpu

F32 = jnp.float32
BF16 = jnp.bfloat16
NORM_EPS = 1e-6
GN_EPS = 64e-5
KK_EPS = 1e-12
CONV_W = 3
V7X_VMEM_LIMIT_BYTES = 56 * 1024 * 1024
LANES = 128
SUBLANES = 8
LOG2E = 1.4426950408889634


def _cparams(*semantics):
    return pltpu.CompilerParams(dimension_semantics=semantics, vmem_limit_bytes=V7X_VMEM_LIMIT_BYTES)


def _const_spec(shape):
    return pl.BlockSpec(shape, lambda *_: (0,) * len(shape), pipeline_mode=pl.Buffered(1))


def _tile(n, pref):
    t = min(n, pref)
    assert n % t == 0, (n, pref)
    return t


def _rms(x, g):
    return x * lax.rsqrt(jnp.mean(x * x, axis=-1, keepdims=True) + NORM_EPS) * g


NN = (((1,), (0,)), ((), ()))
NT = (((1,), (1,)), ((), ()))
TN = (((0,), (0,)), ((), ()))


def _mm(a, b, dims=NN):
    return lax.dot_general(a.astype(BF16), b.astype(BF16), dims, preferred_element_type=F32)


def _split(x):
    hi = x.astype(BF16)
    return hi, (x - hi.astype(F32)).astype(BF16)


def _mm_split_l(x, c):
    hi, lo = _split(x)
    return jnp.dot(hi, c, preferred_element_type=F32) + jnp.dot(lo, c, preferred_element_type=F32)


def _mm_split_r(c, x):
    hi, lo = _split(x)
    return jnp.dot(c, hi, preferred_element_type=F32) + jnp.dot(c, lo, preferred_element_type=F32)


def _softplus_tail(z):
    return jnp.log(1.0 + jnp.exp(-jnp.abs(z)))


def _proj_in_kernel(x_ref, g_ref, wq_ref, wk_ref, wv_ref, wx_ref, xb_ref, *outs, heads, head_major):
    rows = x_ref.shape[0]
    a_w = wq_ref.shape[1]
    hd = a_w // heads
    u = _rms(x_ref[...], g_ref[...]).astype(BF16)
    q = jnp.dot(u, wq_ref[...], preferred_element_type=F32)
    xb_ref[...] = jnp.dot(u, wx_ref[...], preferred_element_type=F32)
    if head_major:
        qh_ref, kt_ref, vt_ref, ktb_ref, vtb_ref = outs
        kt = _mm(wk_ref[...], u, NT).reshape(heads, hd, rows)
        vt = _mm(wv_ref[...], u, NT).reshape(heads, hd, rows)
        kt_ref[...] = kt
        vt_ref[...] = vt
        ktb_ref[...] = kt.astype(BF16)
        vtb_ref[...] = vt.astype(BF16)
        qs = q * (hd ** -0.5 * LOG2E)
        for h in range(heads):
            qh_ref[h] = qs[:, h * hd:(h + 1) * hd].astype(BF16)
    else:
        q_ref, k_ref, v_ref = outs
        q_ref[...] = q
        k_ref[...] = jnp.dot(u, wk_ref[...], preferred_element_type=F32)
        v_ref[...] = jnp.dot(u, wv_ref[...], preferred_element_type=F32)


def _proj_in(x2, g, w_in, *, a_w, heads, seq_rows, head_major):
    n, d = x2.shape
    shift_w = w_in.shape[1] - 3 * a_w
    hd = a_w // heads
    tm = _tile(seq_rows if head_major else n, 512)
    row = lambda i: (i, 0)
    wq, wk, wv, wx = (w_in[:, :a_w], w_in[:, a_w:2 * a_w], w_in[:, 2 * a_w:3 * a_w], w_in[:, 3 * a_w:])
    out_shape = [jax.ShapeDtypeStruct((n, shift_w), F32)]
    out_specs = [pl.BlockSpec((tm, shift_w), row)]
    if head_major:
        bsz, per_seq = n // seq_rows, seq_rows // tm
        wk, wv = wk.T, wv.T
        kv_w_spec = _const_spec((a_w, d))
        t_spec = pl.BlockSpec((None, heads, hd, tm), lambda i: (i // per_seq, 0, 0, i % per_seq))
        out_shape += [jax.ShapeDtypeStruct((bsz, heads, seq_rows, hd), BF16)]
        out_specs += [pl.BlockSpec((None, heads, tm, hd), lambda i: (i // per_seq, 0, i % per_seq, 0))]
        out_shape += [jax.ShapeDtypeStruct((bsz, heads, hd, seq_rows), dt) for dt in (F32, F32, BF16, BF16)]
        out_specs += [t_spec] * 4
    else:
        kv_w_spec = _const_spec((d, a_w))
        out_shape += [jax.ShapeDtypeStruct((n, a_w), F32)] * 3
        out_specs += [pl.BlockSpec((tm, a_w), row)] * 3
    return pl.pallas_call(
        functools.partial(_proj_in_kernel, heads=heads, head_major=head_major),
        grid=(n // tm,),
        in_specs=[pl.BlockSpec((tm, d), row), _const_spec((1, d)), _const_spec((d, a_w)), kv_w_spec, kv_w_spec,
                  _const_spec((d, shift_w))],
        out_specs=out_specs,
        out_shape=out_shape,
        compiler_params=_cparams("parallel"),
        name="proj_in",
    )(x2, g, wq.astype(BF16), wk.astype(BF16), wv.astype(BF16), wx.astype(BF16))


def _sb_scores(z):
    log_beta = jnp.minimum(z, 0.0) - _softplus_tail(z)
    return log_beta, log_beta - z


def _sb_scores2(z2):
    log_beta = jnp.minimum(z2, 0.0) - jnp.log2(1.0 + jnp.exp2(-jnp.abs(z2)))
    return log_beta, log_beta - z2


def _sb_prompt_kernel(bias_ref, q_ref, kt_ref, vt_ref, g_ref, o_ref, acc_ref, carry_ref, *, tq, tk, heads):
    qi = pl.program_id(1)
    per_q = tq // tk
    kj = lax.broadcasted_iota(jnp.int32, (tk, tk), 0)
    ks = lax.broadcasted_iota(jnp.int32, (tk, tk), 1)
    vis = ks < kj
    later = jnp.concatenate([kj > ks, kj >= 0], axis=1).astype(BF16)

    def logits(h, rb, s0):
        return (jnp.dot(q_ref[h, pl.ds(rb * tk, tk), :], kt_ref[h, :, pl.ds(s0, tk)], preferred_element_type=F32)
                + bias_ref[h])

    def keep_sums(z, diagonal):
        log_beta, log_keep = _sb_scores2(z)
        if diagonal:
            log_keep = jnp.where(vis, log_keep, 0.0)
        return log_beta, jnp.dot(log_keep.astype(BF16), later, preferred_element_type=F32)

    def weigh(h, rb, s0, diagonal, log_beta, sums):
        rs = pl.ds(rb * tk, tk)
        a = jnp.exp2(log_beta + sums[:, :tk] + (0.0 if diagonal else carry_ref[h, rs, :]))
        if diagonal:
            a = jnp.where(vis, a, 0.0)
        pv = _mm(a, vt_ref[h, :, pl.ds(s0, tk)], NT)
        if diagonal:
            acc_ref[h, rs, :] = pv
            carry_ref[h, rs, :] = sums[:, tk:]
        else:
            acc_ref[h, rs, :] += pv
            carry_ref[h, rs, :] += sums[:, tk:]

    def visit(s0, own):
        chains = [(h, rb) for h in range(heads) for rb in range(per_q) if own is None or rb >= own]
        n = len(chains)
        z, mid = {}, {}
        for t in range(n + 2):
            if t < n:
                z[t] = logits(*chains[t], s0)
            if 0 <= t - 1 < n:
                mid[t - 1] = keep_sums(z.pop(t - 1), chains[t - 1][1] == own)
            if 0 <= t - 2 < n:
                h, rb = chains[t - 2]
                weigh(h, rb, s0, rb == own, *mid.pop(t - 2))

    for own in reversed(range(per_q)):
        visit(pl.multiple_of(qi * tq + own * tk, tk), own)

    def earlier(j, _):
        visit(pl.multiple_of((qi * per_q - 1 - j) * tk, tk), None)
        return 0

    lax.fori_loop(0, qi * per_q, earlier, 0)
    o_ref[...] = jnp.concatenate([_rms(acc_ref[h], g_ref[h:h + 1, :]) for h in range(heads)],
                                 axis=-1).astype(o_ref.dtype)


def _sb_prompt(qh, ktb, vtb, bias, g_att):
    bsz, heads, t, hd = qh.shape
    tq = _tile(t, 4 * LANES)
    tk = _tile(tq, 2 * LANES)
    kv_spec = pl.BlockSpec((None, heads, hd, t), lambda b, i: (b, 0, 0, 0))
    return pl.pallas_call(
        functools.partial(_sb_prompt_kernel, tq=tq, tk=tk, heads=heads),
        grid=(bsz, t // tq),
        in_specs=[pl.BlockSpec(memory_space=pltpu.SMEM),
                  pl.BlockSpec((None, heads, tq, hd), lambda b, i: (b, 0, i, 0)),
                  kv_spec, kv_spec, _const_spec((heads, hd))],
        out_specs=pl.BlockSpec((tq, heads * hd), lambda b, i: (b * (t // tq) + i, 0)),
        out_shape=jax.ShapeDtypeStruct((bsz * t, heads * hd), BF16),
        scratch_shapes=[pltpu.VMEM((heads, tq, hd), F32), pltpu.VMEM((heads, tq, tk), F32)],
        compiler_params=_cparams("parallel", "arbitrary"),
        name="sb_prompt",
    )(bias, qh, ktb, vtb, g_att)


def _sb_sample_kernel(pt_ref, bias_ref, q_ref, kn_ref, vn_ref, g_ref, *rest, heads, pp):
    del pt_ref
    k_refs, v_refs = rest[:pp], rest[pp:2 * pp]
    o_ref = rest[2 * pp]
    qbd_ref, bcol_ref, acc_ref, carry_ref = rest[2 * pp + 1:]
    p = pl.program_id(1)
    t_new, a_w = q_ref.shape
    hd = a_w // heads
    page = k_refs[0].shape[-1]
    lane = lax.broadcasted_iota(jnp.int32, (t_new, a_w), 1)
    own = jnp.concatenate([(lane >= h * hd) & (lane < (h + 1) * hd) for h in range(heads)], axis=0)

    def later_mask(m):
        return (lax.broadcasted_iota(jnp.int32, (m, m), 0) > lax.broadcasted_iota(jnp.int32, (m, m), 1)).astype(BF16)

    @pl.when(p == 0)
    def _():
        q = q_ref[...] * (hd ** -0.5)
        qbd_ref[...] = jnp.where(own, jnp.concatenate([q] * heads, axis=0), 0.0).astype(BF16)
        bcol_ref[...] = jnp.concatenate([jnp.full((t_new, 1), bias_ref[h], F32) for h in range(heads)], axis=0)
        z = _mm(qbd_ref[...], kn_ref[...], NT) + bcol_ref[...]
        vis1 = (lax.broadcasted_iota(jnp.int32, (t_new, t_new), 1)
                < lax.broadcasted_iota(jnp.int32, (t_new, t_new), 0))
        vis = jnp.concatenate([vis1] * heads, axis=0)
        log_beta, log_keep = _sb_scores(z)
        log_keep = jnp.where(vis, log_keep, 0.0)
        rest_ = jnp.dot(log_keep.astype(BF16), later_mask(t_new), preferred_element_type=F32)
        a = jnp.where(vis, jnp.exp(log_beta + rest_), 0.0)
        acc_ref[...] = _mm(a, vn_ref[...])
        carry_ref[...] = jnp.broadcast_to(jnp.sum(log_keep, axis=-1, keepdims=True), carry_ref.shape)

    ones = lax.broadcasted_iota(jnp.int32, (page, page), 0) >= 0
    later = jnp.concatenate([later_mask(page), ones.astype(BF16)], axis=1)
    js = range(pp)
    qbd, bcol = qbd_ref[...], bcol_ref[...]
    z = [_mm(qbd, k_refs[j][...].reshape(a_w, page)) + bcol for j in js]
    scores = [_sb_scores(z[j]) for j in js]
    sums = [jnp.dot(scores[j][1].astype(BF16), later, preferred_element_type=F32) for j in js]
    carry = carry_ref[...]
    a = []
    for j in js:
        a.append(jnp.exp(scores[j][0] + sums[j][:, :page] + carry))
        carry = carry + sums[j][:, page:]
    carry_ref[...] = carry
    acc = acc_ref[...]
    for j in js:
        acc = acc + _mm(a[j], v_refs[j][...].reshape(a_w, page), NT)
    acc_ref[...] = acc

    @pl.when(p == pl.num_programs(1) - 1)
    def _():
        att = jnp.where(own, acc, 0.0).reshape(heads, t_new, a_w).sum(axis=0)
        o_ref[...] = jnp.concatenate([_rms(att[:, h * hd:(h + 1) * hd], g_ref[h:h + 1, :]) for h in range(heads)],
                                     axis=-1).astype(o_ref.dtype)


def _sb_sample(q2, k2, v2, cache_kt, cache_vt, page_table, bias, g_att, *, t_new):
    n, a_w = q2.shape
    heads, hd = g_att.shape
    bsz = n // t_new
    page = cache_kt.shape[-1]
    n_pages = page_table.shape[1]
    pp = _tile(n_pages, 16)

    def page_spec(j):
        return pl.BlockSpec((None, heads, hd, page), lambda b, p, pt: (pt[b, n_pages - 1 - p * pp - j], 0, 0, 0))

    tok = pl.BlockSpec((t_new, a_w), lambda b, p, pt: (b, 0))
    grid_spec = pltpu.PrefetchScalarGridSpec(
        num_scalar_prefetch=1,
        grid=(bsz, n_pages // pp),
        in_specs=[pl.BlockSpec(memory_space=pltpu.SMEM), tok, tok, tok,
                  pl.BlockSpec((heads, hd), lambda b, p, pt: (0, 0))] + [page_spec(j) for j in range(pp)] * 2,
        out_specs=tok,
        scratch_shapes=[pltpu.VMEM((heads * t_new, a_w), BF16), pltpu.VMEM((heads * t_new, 1), F32),
                        pltpu.VMEM((heads * t_new, a_w), F32), pltpu.VMEM((heads * t_new, page), F32)],
    )
    return pl.pallas_call(
        functools.partial(_sb_sample_kernel, heads=heads, pp=pp),
        grid_spec=grid_spec,
        out_shape=jax.ShapeDtypeStruct((n, a_w), BF16),
        compiler_params=_cparams("parallel", "arbitrary"),
        name="sb_sample",
    )(page_table, bias, q2, k2, v2, g_att, *([cache_kt] * pp), *([cache_vt] * pp))


def _rwkv_kernel(xb_ref, shift0_ref, wkv0_ref, mu_ref, wd_ref, wa_ref, wg_ref, dbase_ref, abase_ref,
                 kk_ref, ka_ref, rk_ref, lnw_ref, lnb_ref, gsum_ref, ltri_ref,
                 tm_ref, wkv_ref, s_ref, prev_ref, *, heads):
    c = pl.program_id(1)
    nseq, rows, shift_w = xb_ref.shape
    n = nseq * rows
    b_w = tm_ref.shape[-1]
    hd = b_w // heads

    @pl.when(c == 0)
    def _():
        s_ref[...] = wkv0_ref[...]
        prev_ref[...] = shift0_ref[...]

    xb = xb_ref[...].reshape(n, shift_w)
    rowi = lax.broadcasted_iota(jnp.int32, (n, 1), 0)
    prev = pltpu.roll(xb, 1, axis=0)
    for s in range(nseq):
        prev = jnp.where(rowi == s * rows, prev_ref[s], prev)
        prev_ref[s] = xb[(s + 1) * rows - 1:(s + 1) * rows, :]
    xs = xb + (prev - xb) * mu_ref[...]
    r = xs[:, 0:b_w]
    k = xs[:, b_w:2 * b_w]
    v = xs[:, 2 * b_w:3 * b_w]
    tail = xs[:, 3 * b_w:]

    dec_in = dbase_ref[...] + _mm(jnp.tanh(tail), wd_ref[...])
    w_raw = -(jnp.maximum(-dec_in, 0.0) + _softplus_tail(dec_in)) - 0.5
    logw = -jnp.exp(w_raw)
    a_lr = jax.nn.sigmoid(abase_ref[...] + _mm(tail, wa_ref[...]))
    gate = _mm(jax.nn.sigmoid(tail), wg_ref[...])
    kk = k * kk_ref[...]
    k2 = k * (1.0 + (a_lr - 1.0) * ka_ref[...])
    kk = kk / jnp.maximum(jnp.sqrt(_mm_split_l(kk * kk, gsum_ref[...])), KK_EPS)

    cum = _mm_split_r(ltri_ref[...], logw)
    e_pos = jnp.exp(cum)
    e_neg = jnp.exp(-cum)
    a_hat = -kk * jnp.exp(cum - logw)
    b_hat = kk * a_lr * e_neg
    k_hat = k2 * e_neg
    r_hat = r * e_pos

    ri = lax.broadcasted_iota(jnp.int32, (rows, rows), 0)
    ci = lax.broadcasted_iota(jnp.int32, (rows, rows), 1)
    strict = ci < ri
    incl = ci <= ri
    eye = (ci == ri).astype(F32)
    doublings = max(1, (rows - 1).bit_length())

    ch = [(s, h) for s in range(nseq) for h in range(heads)]
    cut = lambda x, s, h: x[s * rows:(s + 1) * rows, h * hd:(h + 1) * hd]
    s0 = {(s, h): s_ref[s, h] for s, h in ch}
    s0b = {i: s0[i].astype(BF16) for i in ch}
    a_b, b_b, k_b, r_b, v_b = (x.astype(BF16) for x in (a_hat, b_hat, k_hat, r_hat, v))
    ah = {i: cut(a_b, *i) for i in ch}
    bh = {i: cut(b_b, *i) for i in ch}
    kh = {i: cut(k_b, *i) for i in ch}
    rh = {i: cut(r_b, *i) for i in ch}
    vh = {i: cut(v_b, *i) for i in ch}
    m = {i: _mm(jnp.concatenate([ah[i], rh[i]], axis=0), jnp.concatenate([bh[i], kh[i]], axis=0), NT) for i in ch}
    a_ab = {i: jnp.where(strict, m[i][0:rows, 0:rows], 0.0) for i in ch}
    a_ak = {i: jnp.where(strict, m[i][0:rows, rows:], 0.0) for i in ch}
    a_rb = {i: jnp.where(incl, m[i][rows:, 0:rows], 0.0) for i in ch}
    a_rk = {i: jnp.where(incl, m[i][rows:, rows:], 0.0) for i in ch}
    u_rhs = {i: _mm(ah[i], s0b[i], NT) + _mm(a_ak[i], vh[i]) for i in ch}
    y_part = {i: _mm(rh[i], s0b[i], NT) + _mm(a_rk[i], vh[i]) for i in ch}
    s_part = {i: s0[i] + _mm(vh[i], kh[i], TN) for i in ch}
    inv = {i: eye + a_ab[i] for i in ch}
    pw = a_ab
    for _ in range(doublings - 1):
        pw = {i: _mm(pw[i], pw[i]) for i in ch}
        inv = {i: inv[i] + _mm(inv[i], pw[i]) for i in ch}
    u = {i: _mm(inv[i], u_rhs[i]) for i in ch}
    ys = {i: y_part[i] + _mm(a_rb[i], u[i]) for i in ch}
    for s, h in ch:
        g_end = e_pos[(s + 1) * rows - 1:(s + 1) * rows, h * hd:(h + 1) * hd]
        s_ref[s, h] = (s_part[s, h] + _mm(u[s, h], bh[s, h], TN)) * g_end

    y = jnp.concatenate([jnp.concatenate([ys[s, h] for h in range(heads)], axis=-1) for s in range(nseq)], axis=0)
    gmean = gsum_ref[...]
    mean = _mm_split_l(y, gmean) * (1.0 / hd)
    d = y - mean
    var = _mm_split_l(d * d, gmean) * (1.0 / hd)
    yn = d * lax.rsqrt(var + GN_EPS) * lnw_ref[...] + lnb_ref[...]
    bonus = _mm_split_l(r * k2 * rk_ref[...], gmean) * v
    tm_ref[...] = ((yn + bonus) * gate).astype(tm_ref.dtype).reshape(nseq, rows, b_w)

    @pl.when(c == pl.num_programs(1) - 1)
    def _():
        wkv_ref[...] = s_ref[...]


def _rwkv(xb2, shift0, wkv0, lp, *, seq_rows):
    n, shift_w = xb2.shape
    bsz, heads, hd, _ = wkv0.shape
    b_w = heads * hd
    tail_w = shift_w - 3 * b_w
    rows = _tile(seq_rows, 64)
    per_seq = seq_rows // rows
    nseq = _tile(bsz, 4)
    r_dec, r_a, r_gate = lp['w_decay_up'].shape[0], lp['w_a_up'].shape[0], lp['w_g_up'].shape[0]
    assert r_dec + r_a + r_gate == tail_w

    def pad_rows(w, start):
        return jnp.zeros((tail_w, b_w), BF16).at[start:start + w.shape[0]].set(w.astype(BF16))

    wd = pad_rows(lp['w_decay_up'], 0)
    wa = pad_rows(lp['w_a_up'], r_dec)
    wg = pad_rows(lp['w_g_up'], r_dec + r_a)
    ch = jnp.arange(b_w)
    gsum = (ch[:, None] // hd == ch[None, :] // hd).astype(BF16)
    pos = jnp.arange(nseq * rows)
    ltri = ((pos[None, :] <= pos[:, None]) & (pos[None, :] // rows == pos[:, None] // rows)).astype(BF16)
    vec = lambda a: a.reshape(1, -1).astype(F32)
    state_spec = pl.BlockSpec((nseq, heads, hd, hd), lambda b, c: (b, 0, 0, 0))
    vb = _const_spec((1, b_w))
    tmix, wkv_new = pl.pallas_call(
        functools.partial(_rwkv_kernel, heads=heads),
        grid=(bsz // nseq, per_seq),
        in_specs=[pl.BlockSpec((nseq, rows, shift_w), lambda b, c: (b, c, 0)),
                  pl.BlockSpec((nseq, 1, shift_w), lambda b, c: (b, 0, 0)),
                  state_spec, _const_spec((1, shift_w)),
                  _const_spec((tail_w, b_w)), _const_spec((tail_w, b_w)), _const_spec((tail_w, b_w)),
                  vb, vb, vb, vb, vb, vb, vb, _const_spec((b_w, b_w)), _const_spec((nseq * rows, nseq * rows))],
        out_specs=[pl.BlockSpec((nseq, rows, b_w), lambda b, c: (b, c, 0)), state_spec],
        out_shape=[jax.ShapeDtypeStruct((bsz, seq_rows, b_w), BF16), jax.ShapeDtypeStruct(wkv0.shape, F32)],
        scratch_shapes=[pltpu.VMEM((nseq, heads, hd, hd), F32), pltpu.VMEM((nseq, 1, shift_w), F32)],
        compiler_params=_cparams("parallel", "arbitrary"),
        name="rwkv",
    )(xb2.reshape(bsz, seq_rows, shift_w), shift0.reshape(bsz, 1, shift_w), wkv0, vec(lp['mu_shift']), wd, wa, wg,
      vec(lp['decay_base']), vec(lp['a_base']), vec(lp['k_k']), vec(lp['k_a']), vec(lp['r_k']), vec(lp['lnx_w']),
      vec(lp['lnx_b']), gsum, ltri)
    return tmix.reshape(n, b_w), wkv_new


def _ffn_kernel(x_ref, att_ref, tmix_ref, pe_ref, st0_ref, st1_ref, woa_ref, wob_ref, gffn_ref, wg_ref, wv_ref,
                cw_ref, cb_ref, wfo_ref, wpi_ref, gple_ref, wpg_ref, gfin_ref, y_ref, tail_ref,
                c0_ref, c1_ref, *, seq_rows, final_norm):
    rows = x_ref.shape[0]
    carried = seq_rows >= rows
    h = (x_ref[...] + jnp.dot(att_ref[...], woa_ref[...], preferred_element_type=F32)
         + jnp.dot(tmix_ref[...], wob_ref[...], preferred_element_type=F32))
    un = _rms(h, gffn_ref[...]).astype(BF16)
    gp = jnp.dot(un, wg_ref[...], preferred_element_type=F32)
    val = jnp.dot(un, wv_ref[...], preferred_element_type=F32)

    rowi = lax.broadcasted_iota(jnp.int32, (rows, 1), 0)
    if carried:
        tpos = rowi
        first = pl.program_id(0) % (seq_rows // rows) == 0

        @pl.when(first)
        def _():
            c0_ref[...] = st0_ref[...]
            c1_ref[...] = st1_ref[...]

        h0, h1 = c0_ref[...], c1_ref[...]
    else:
        assert seq_rows & (seq_rows - 1) == 0
        tpos = jnp.bitwise_and(rowi, seq_rows - 1)
        h0, h1 = st0_ref[...], st1_ref[...]
    prev1 = jnp.where(tpos == 0, h1, pltpu.roll(gp, 1, axis=0))
    prev2 = jnp.where(tpos == 0, h0, jnp.where(tpos == 1, h1, pltpu.roll(gp, 2, axis=0)))
    cw = cw_ref[...]
    cpre = cb_ref[...] + prev2 * cw[0:1, :] + prev1 * cw[1:2, :] + gp * cw[2:3, :]
    act = cpre * jax.nn.sigmoid(cpre) * val
    if carried:
        c0_ref[...] = gp[rows - 2:rows - 1, :]
        c1_ref[...] = gp[rows - 1:rows, :]
    tail_rows = tail_ref.shape[0]
    tail_ref[...] = gp[rows - tail_rows:rows, :]

    h = h + _mm(act, wfo_ref[...])
    pe_proj = _rms(_mm(pe_ref[...], wpi_ref[...]), gple_ref[...])
    h = h + jax.nn.sigmoid(_mm(h, wpg_ref[...])) * pe_proj
    y_ref[...] = _rms(h, gfin_ref[...]) if final_norm else h


def _ffn(x2, att, tmix, pe2, st0, st1, lp, g_final, *, seq_rows, final_norm):
    n, d = x2.shape
    a_w = att.shape[1]
    d_ff = lp['conv_w'].shape[1]
    ple = pe2.shape[1]
    assert lp['conv_w'].shape[0] == CONV_W
    bf = lambda a: a.astype(BF16)
    vec = lambda a: a.reshape(1, -1).astype(F32)
    row = lambda i: (i, 0)
    if seq_rows >= 256:
        tm = _tile(seq_rows, 256)
        per_seq = seq_rows // tm
        assert tm >= SUBLANES and seq_rows >= CONV_W - 1
        st_spec = pl.BlockSpec((None, 1, d_ff), lambda i: (i // per_seq, 0, 0))
        tail_rows = SUBLANES
        tail_shape = jax.ShapeDtypeStruct((n // seq_rows, tail_rows, d_ff), F32)
        tail_spec = pl.BlockSpec((None, tail_rows, d_ff), lambda i: (i // per_seq, 0, 0))
        semantics = "arbitrary"
    else:
        tm = _tile(n, 256)
        assert tm % seq_rows == 0
        st_spec = pl.BlockSpec((tm, d_ff), row)
        tail_rows = tm
        tail_shape = jax.ShapeDtypeStruct((n, d_ff), F32)
        tail_spec = pl.BlockSpec((tm, d_ff), row)
        semantics = "parallel"
    w_in = lp['w_ffn_in']
    return pl.pallas_call(
        functools.partial(_ffn_kernel, seq_rows=seq_rows, final_norm=final_norm),
        grid=(n // tm,),
        in_specs=[pl.BlockSpec((tm, d), row), pl.BlockSpec((tm, a_w), row), pl.BlockSpec((tm, tmix.shape[1]), row),
                  pl.BlockSpec((tm, ple), row), st_spec, st_spec,
                  _const_spec((a_w, d)), _const_spec((tmix.shape[1], d)), _const_spec((1, d)),
                  _const_spec((d, d_ff)), _const_spec((d, d_ff)), _const_spec((CONV_W, d_ff)), _const_spec((1, d_ff)),
                  _const_spec((d_ff, d)), _const_spec((ple, d)), _const_spec((1, d)), _const_spec((d, d)),
                  _const_spec((1, d))],
        out_specs=[pl.BlockSpec((tm, d), row), tail_spec],
        out_shape=[jax.ShapeDtypeStruct((n, d), F32), tail_shape],
        scratch_shapes=[pltpu.VMEM((1, d_ff), F32), pltpu.VMEM((1, d_ff), F32)],
        compiler_params=_cparams(semantics),
        name="ffn",
    )(x2, att, tmix, pe2, st0, st1, bf(lp['w_out'][:a_w]), bf(lp['w_out'][a_w:]), vec(lp['g_ffn']),
      bf(w_in[:, :d_ff]), bf(w_in[:, d_ff:]), lp['conv_w'].astype(F32), vec(lp['conv_b']), bf(lp['w_ffn_out']),
      bf(lp['w_ple_in']), vec(lp['g_ple']), bf(lp['w_ple_gate']), vec(g_final))


def _layer(x, pe, wkv0, shift0, conv0, lp, g_final, final_norm, past):
    bsz, t, d = x.shape
    heads, hd = lp['g_att'].shape
    a_w = heads * hd
    n = bsz * t
    x2 = x.reshape(n, d)
    prompt = past is None
    bias, g_att = lp['sb_bias'].astype(F32), lp['g_att'].astype(F32)
    xb2, *proj = _proj_in(x2, lp['g_mix'].reshape(1, d).astype(F32), lp['w_in'], a_w=a_w, heads=heads,
                          seq_rows=t, head_major=prompt)
    if prompt:
        qh, kt, vt, ktb, vtb = proj
        att = _sb_prompt(qh, ktb, vtb, bias * LOG2E, g_att)
        k_new, v_new = jnp.transpose(kt, (0, 3, 1, 2)), jnp.transpose(vt, (0, 3, 1, 2))
    else:
        q2, k2, v2 = proj
        cache_k, cache_v, page_table = past
        cache_kt, cache_vt = jnp.transpose(cache_k, (0, 2, 3, 1)), jnp.transpose(cache_v, (0, 2, 3, 1))
        att = _sb_sample(q2, k2, v2, cache_kt, cache_vt, page_table, bias, g_att, t_new=t)
        k_new, v_new = k2.reshape(bsz, t, heads, hd), v2.reshape(bsz, t, heads, hd)
    tmix, wkv_new = _rwkv(xb2, shift0, wkv0, lp, seq_rows=t)
    d_ff = conv0.shape[-1]
    if t >= 256:
        st0, st1 = conv0[:, 0:1, :], conv0[:, 1:2, :]
    else:
        st0 = jnp.repeat(conv0[:, 0, :], t, axis=0)
        st1 = jnp.repeat(conv0[:, 1, :], t, axis=0)
    y2, tail = _ffn(x2, att, tmix, pe.reshape(n, -1), st0, st1, lp, g_final, seq_rows=t, final_norm=final_norm)
    conv_new = tail.reshape(bsz, -1, d_ff)[:, -(CONV_W - 1):, :]
    return (y2.reshape(bsz, t, d), k_new, v_new, wkv_new, xb2.reshape(bsz, t, -1)[:, -1, :], conv_new)


def kernel(x_prompt, x_sample, cache_k, cache_v, state_wkv, state_shift, state_conv, page_table, p_prompt, p_sample, g_mix, w_in, mu_shift, w_decay_up, decay_base, w_a_up, a_base, w_g_up, k_k, k_a, r_k, lnx_w, lnx_b, g_att, sb_bias, w_out, g_ffn, w_ffn_in, conv_w, conv_b, w_ffn_out, w_ple_in, g_ple, w_ple_gate, g_final):
    depth = g_mix.shape[0]
    bp = x_prompt.shape[0]
    heads_b, hd = r_k.shape[1], r_k.shape[2]
    hp, hs = x_prompt, x_sample
    outs_p, outs_s = [], []
    for i in range(depth):
        lp = dict(g_mix=g_mix[i], w_in=w_in[i], mu_shift=mu_shift[i], w_decay_up=w_decay_up[i],
                  decay_base=decay_base[i], w_a_up=w_a_up[i], a_base=a_base[i], w_g_up=w_g_up[i], k_k=k_k[i],
                  k_a=k_a[i], r_k=r_k[i], lnx_w=lnx_w[i], lnx_b=lnx_b[i], g_att=g_att[i], sb_bias=sb_bias[i],
                  w_out=w_out[i], g_ffn=g_ffn[i], w_ffn_in=w_ffn_in[i], conv_w=conv_w[i], conv_b=conv_b[i],
                  w_ffn_out=w_ffn_out[i], w_ple_in=w_ple_in[i], g_ple=g_ple[i], w_ple_gate=w_ple_gate[i])
        last = i == depth - 1
        hp, *rest_p = _layer(hp, p_prompt[i], jnp.zeros((bp, heads_b, hd, hd), F32),
                             jnp.zeros((bp, state_shift.shape[-1]), F32),
                             jnp.zeros((bp,) + state_conv.shape[2:], F32), lp, g_final, last, None)
        hs, *rest_s = _layer(hs, p_sample[i], state_wkv[i], state_shift[i], state_conv[i], lp, g_final, last,
                             (cache_k[i], cache_v[i], page_table))
        outs_p.append(rest_p)
        outs_s.append(rest_s)
    stack = lambda outs, j: jnp.stack([o[j] for o in outs])
    return (hp, hs) + tuple(stack(outs_p, j) for j in range(5)) + tuple(stack(outs_s, j) for j in range(5))
```

```python
import functools

import jax
import jax.numpy as jnp
from jax import lax
from jax.experimental import pallas as pl
from jax.experimental.pallas import tpu as pltpu

F32 = jnp.float32
BF16 = jnp.bfloat16
NORM_EPS = 1e-6
GN_EPS = 64e-5
KK_EPS = 1e-12
CONV_W = 3
V7X_VMEM_LIMIT_BYTES = 56 * 1024 * 1024
LANES = 128
SUBLANES = 8
LOG2E = 1.4426950408889634
FFN_TILE_ROWS = 256
FFN_CHAIN_ROWS = 128


def _cparams(*semantics):
    return pltpu.CompilerParams(dimension_semantics=semantics, vmem_limit_bytes=V7X_VMEM_LIMIT_BYTES)


def _const_spec(shape):
    return pl.BlockSpec(shape, lambda *_: (0,) * len(shape), pipeline_mode=pl.Buffered(1))


def _tile(n, pref):
    t = min(n, pref)
    assert n % t == 0, (n, pref)
    return t


def _rms(x, g):
    return x * lax.rsqrt(jnp.mean(x * x, axis=-1, keepdims=True) + NORM_EPS) * g


NN = (((1,), (0,)), ((), ()))
NT = (((1,), (1,)), ((), ()))
TN = (((0,), (0,)), ((), ()))


def _mm(a, b, dims=NN):
    return lax.dot_general(a.astype(BF16), b.astype(BF16), dims, preferred_element_type=F32)


def _split(x):
    hi = x.astype(BF16)
    return hi, (x - hi.astype(F32)).astype(BF16)


def _mm_split_l(x, c):
    hi, lo = _split(x)
    return jnp.dot(hi, c, preferred_element_type=F32) + jnp.dot(lo, c, preferred_element_type=F32)


def _mm_split_r(c, x):
    hi, lo = _split(x)
    return jnp.dot(c, hi, preferred_element_type=F32) + jnp.dot(c, lo, preferred_element_type=F32)


def _softplus_tail(z):
    return jnp.log(1.0 + jnp.exp(-jnp.abs(z)))


def _proj_in_kernel(x_ref, g_ref, wq_ref, wk_ref, wv_ref, wx_ref, xb_ref, *outs, heads, head_major):
    rows = x_ref.shape[0]
    a_w = wq_ref.shape[1]
    hd = a_w // heads
    u = _rms(x_ref[...], g_ref[...]).astype(BF16)
    q = jnp.dot(u, wq_ref[...], preferred_element_type=F32)
    xb_ref[...] = jnp.dot(u, wx_ref[...], preferred_element_type=F32)
    if head_major:
        qh_ref, kt_ref, vt_ref, ktb_ref, vtb_ref = outs
        kt = _mm(wk_ref[...], u, NT).reshape(heads, hd, rows)
        vt = _mm(wv_ref[...], u, NT).reshape(heads, hd, rows)
        kt_ref[...] = kt
        vt_ref[...] = vt
        ktb_ref[...] = kt.astype(BF16)
        vtb_ref[...] = vt.astype(BF16)
        qs = q * (hd ** -0.5 * LOG2E)
        for h in range(heads):
            qh_ref[h] = qs[:, h * hd:(h + 1) * hd].astype(BF16)
    else:
        q_ref, k_ref, v_ref = outs
        q_ref[...] = q
        k_ref[...] = jnp.dot(u, wk_ref[...], preferred_element_type=F32)
        v_ref[...] = jnp.dot(u, wv_ref[...], preferred_element_type=F32)


def _proj_in(x2, g, w_in, *, a_w, heads, seq_rows, head_major):
    n, d = x2.shape
    shift_w = w_in.shape[1] - 3 * a_w
    hd = a_w // heads
    tm = _tile(seq_rows if head_major else n, 512)
    row = lambda i: (i, 0)
    wq, wk, wv, wx = (w_in[:, :a_w], w_in[:, a_w:2 * a_w], w_in[:, 2 * a_w:3 * a_w], w_in[:, 3 * a_w:])
    out_shape = [jax.ShapeDtypeStruct((n, shift_w), F32)]
    out_specs = [pl.BlockSpec((tm, shift_w), row)]
    if head_major:
        bsz, per_seq = n // seq_rows, seq_rows // tm
        wk, wv = wk.T, wv.T
        kv_w_spec = _const_spec((a_w, d))
        t_spec = pl.BlockSpec((None, heads, hd, tm), lambda i: (i // per_seq, 0, 0, i % per_seq))
        out_shape += [jax.ShapeDtypeStruct((bsz, heads, seq_rows, hd), BF16)]
        out_specs += [pl.BlockSpec((None, heads, tm, hd), lambda i: (i // per_seq, 0, i % per_seq, 0))]
        out_shape += [jax.ShapeDtypeStruct((bsz, heads, hd, seq_rows), dt) for dt in (F32, F32, BF16, BF16)]
        out_specs += [t_spec] * 4
    else:
        kv_w_spec = _const_spec((d, a_w))
        out_shape += [jax.ShapeDtypeStruct((n, a_w), F32)] * 3
        out_specs += [pl.BlockSpec((tm, a_w), row)] * 3
    return pl.pallas_call(
        functools.partial(_proj_in_kernel, heads=heads, head_major=head_major),
        grid=(n // tm,),
        in_specs=[pl.BlockSpec((tm, d), row), _const_spec((1, d)), _const_spec((d, a_w)), kv_w_spec, kv_w_spec,
                  _const_spec((d, shift_w))],
        out_specs=out_specs,
        out_shape=out_shape,
        compiler_params=_cparams("parallel"),
        name="proj_in",
    )(x2, g, wq.astype(BF16), wk.astype(BF16), wv.astype(BF16), wx.astype(BF16))


def _sb_scores(z):
    log_beta = jnp.minimum(z, 0.0) - _softplus_tail(z)
    return log_beta, log_beta - z


def _sb_scores2(z2):
    log_beta = jnp.minimum(z2, 0.0) - jnp.log2(1.0 + jnp.exp2(-jnp.abs(z2)))
    return log_beta, log_beta - z2


def _sb_prompt_kernel(bias_ref, q_ref, kt_ref, vt_ref, g_ref, o_ref, acc_ref, carry_ref, *, tq, tk, heads):
    qi = pl.program_id(1)
    per_q = tq // tk
    kj = lax.broadcasted_iota(jnp.int32, (tk, tk), 0)
    ks = lax.broadcasted_iota(jnp.int32, (tk, tk), 1)
    vis = ks < kj
    later = jnp.concatenate([kj > ks, kj >= 0], axis=1).astype(BF16)

    def logits(h, rb, s0):
        return (jnp.dot(q_ref[h, pl.ds(rb * tk, tk), :], kt_ref[h, :, pl.ds(s0, tk)], preferred_element_type=F32)
                + bias_ref[h])

    def keep_sums(z, diagonal):
        log_beta, log_keep = _sb_scores2(z)
        if diagonal:
            log_keep = jnp.where(vis, log_keep, 0.0)
        return log_beta, jnp.dot(log_keep.astype(BF16), later, preferred_element_type=F32)

    def weigh(h, rb, s0, diagonal, log_beta, sums):
        rs = pl.ds(rb * tk, tk)
        a = jnp.exp2(log_beta + sums[:, :tk] + (0.0 if diagonal else carry_ref[h, rs, :]))
        if diagonal:
            a = jnp.where(vis, a, 0.0)
        pv = _mm(a, vt_ref[h, :, pl.ds(s0, tk)], NT)
        if diagonal:
            acc_ref[h, rs, :] = pv
            carry_ref[h, rs, :] = sums[:, tk:]
        else:
            acc_ref[h, rs, :] += pv
            carry_ref[h, rs, :] += sums[:, tk:]

    def visit(s0, own):
        chains = [(h, rb) for h in range(heads) for rb in range(per_q) if own is None or rb >= own]
        n = len(chains)
        z, mid = {}, {}
        for t in range(n + 2):
            if t < n:
                z[t] = logits(*chains[t], s0)
            if 0 <= t - 1 < n:
                mid[t - 1] = keep_sums(z.pop(t - 1), chains[t - 1][1] == own)
            if 0 <= t - 2 < n:
                h, rb = chains[t - 2]
                weigh(h, rb, s0, rb == own, *mid.pop(t - 2))

    for own in reversed(range(per_q)):
        visit(pl.multiple_of(qi * tq + own * tk, tk), own)

    def earlier(j, _):
        visit(pl.multiple_of((qi * per_q - 1 - j) * tk, tk), None)
        return 0

    lax.fori_loop(0, qi * per_q, earlier, 0)
    o_ref[...] = jnp.concatenate([_rms(acc_ref[h], g_ref[h:h + 1, :]) for h in range(heads)],
                                 axis=-1).astype(o_ref.dtype)


def _sb_prompt(qh, ktb, vtb, bias, g_att):
    bsz, heads, t, hd = qh.shape
    tq = _tile(t, 4 * LANES)
    tk = _tile(tq, 2 * LANES)
    kv_spec = pl.BlockSpec((None, heads, hd, t), lambda b, i: (b, 0, 0, 0))
    return pl.pallas_call(
        functools.partial(_sb_prompt_kernel, tq=tq, tk=tk, heads=heads),
        grid=(bsz, t // tq),
        in_specs=[pl.BlockSpec(memory_space=pltpu.SMEM),
                  pl.BlockSpec((None, heads, tq, hd), lambda b, i: (b, 0, i, 0)),
                  kv_spec, kv_spec, _const_spec((heads, hd))],
        out_specs=pl.BlockSpec((tq, heads * hd), lambda b, i: (b * (t // tq) + i, 0)),
        out_shape=jax.ShapeDtypeStruct((bsz * t, heads * hd), BF16),
        scratch_shapes=[pltpu.VMEM((heads, tq, hd), F32), pltpu.VMEM((heads, tq, tk), F32)],
        compiler_params=_cparams("parallel", "arbitrary"),
        name="sb_prompt",
    )(bias, qh, ktb, vtb, g_att)


def _sb_sample_kernel(pt_ref, bias_ref, q_ref, kn_ref, vn_ref, g_ref, *rest, heads, pp):
    del pt_ref
    k_refs, v_refs = rest[:pp], rest[pp:2 * pp]
    o_ref = rest[2 * pp]
    qbd_ref, bcol_ref, acc_ref, carry_ref = rest[2 * pp + 1:]
    p = pl.program_id(1)
    t_new, a_w = q_ref.shape
    hd = a_w // heads
    page = k_refs[0].shape[-1]
    lane = lax.broadcasted_iota(jnp.int32, (t_new, a_w), 1)
    own = jnp.concatenate([(lane >= h * hd) & (lane < (h + 1) * hd) for h in range(heads)], axis=0)

    def later_mask(m):
        return (lax.broadcasted_iota(jnp.int32, (m, m), 0) > lax.broadcasted_iota(jnp.int32, (m, m), 1)).astype(BF16)

    @pl.when(p == 0)
    def _():
        q = q_ref[...] * (hd ** -0.5)
        qbd_ref[...] = jnp.where(own, jnp.concatenate([q] * heads, axis=0), 0.0).astype(BF16)
        bcol_ref[...] = jnp.concatenate([jnp.full((t_new, 1), bias_ref[h], F32) for h in range(heads)], axis=0)
        z = _mm(qbd_ref[...], kn_ref[...], NT) + bcol_ref[...]
        vis1 = (lax.broadcasted_iota(jnp.int32, (t_new, t_new), 1)
                < lax.broadcasted_iota(jnp.int32, (t_new, t_new), 0))
        vis = jnp.concatenate([vis1] * heads, axis=0)
        log_beta, log_keep = _sb_scores(z)
        log_keep = jnp.where(vis, log_keep, 0.0)
        rest_ = jnp.dot(log_keep.astype(BF16), later_mask(t_new), preferred_element_type=F32)
        a = jnp.where(vis, jnp.exp(log_beta + rest_), 0.0)
        acc_ref[...] = _mm(a, vn_ref[...])
        carry_ref[...] = jnp.broadcast_to(jnp.sum(log_keep, axis=-1, keepdims=True), carry_ref.shape)

    ones = lax.broadcasted_iota(jnp.int32, (page, page), 0) >= 0
    later = jnp.concatenate([later_mask(page), ones.astype(BF16)], axis=1)
    js = range(pp)
    qbd, bcol = qbd_ref[...], bcol_ref[...]
    z = [_mm(qbd, k_refs[j][...].reshape(a_w, page)) + bcol for j in js]
    scores = [_sb_scores(z[j]) for j in js]
    sums = [jnp.dot(scores[j][1].astype(BF16), later, preferred_element_type=F32) for j in js]
    carry = carry_ref[...]
    a = []
    for j in js:
        a.append(jnp.exp(scores[j][0] + sums[j][:, :page] + carry))
        carry = carry + sums[j][:, page:]
    carry_ref[...] = carry
    acc = acc_ref[...]
    for j in js:
        acc = acc + _mm(a[j], v_refs[j][...].reshape(a_w, page), NT)
    acc_ref[...] = acc

    @pl.when(p == pl.num_programs(1) - 1)
    def _():
        att = jnp.where(own, acc, 0.0).reshape(heads, t_new, a_w).sum(axis=0)
        o_ref[...] = jnp.concatenate([_rms(att[:, h * hd:(h + 1) * hd], g_ref[h:h + 1, :]) for h in range(heads)],
                                     axis=-1).astype(o_ref.dtype)


def _sb_sample(q2, k2, v2, cache_kt, cache_vt, page_table, bias, g_att, *, t_new):
    n, a_w = q2.shape
    heads, hd = g_att.shape
    bsz = n // t_new
    page = cache_kt.shape[-1]
    n_pages = page_table.shape[1]
    pp = _tile(n_pages, 16)

    def page_spec(j):
        return pl.BlockSpec((None, heads, hd, page), lambda b, p, pt: (pt[b, n_pages - 1 - p * pp - j], 0, 0, 0))

    tok = pl.BlockSpec((t_new, a_w), lambda b, p, pt: (b, 0))
    grid_spec = pltpu.PrefetchScalarGridSpec(
        num_scalar_prefetch=1,
        grid=(bsz, n_pages // pp),
        in_specs=[pl.BlockSpec(memory_space=pltpu.SMEM), tok, tok, tok,
                  pl.BlockSpec((heads, hd), lambda b, p, pt: (0, 0))] + [page_spec(j) for j in range(pp)] * 2,
        out_specs=tok,
        scratch_shapes=[pltpu.VMEM((heads * t_new, a_w), BF16), pltpu.VMEM((heads * t_new, 1), F32),
                        pltpu.VMEM((heads * t_new, a_w), F32), pltpu.VMEM((heads * t_new, page), F32)],
    )
    return pl.pallas_call(
        functools.partial(_sb_sample_kernel, heads=heads, pp=pp),
        grid_spec=grid_spec,
        out_shape=jax.ShapeDtypeStruct((n, a_w), BF16),
        compiler_params=_cparams("parallel", "arbitrary"),
        name="sb_sample",
    )(page_table, bias, q2, k2, v2, g_att, *([cache_kt] * pp), *([cache_vt] * pp))


def _rwkv_kernel(xb_ref, shift0_ref, wkv0_ref, mu_ref, wd_ref, wa_ref, wg_ref, dbase_ref, abase_ref,
                 kk_ref, ka_ref, rk_ref, lnw_ref, lnb_ref, gsum_ref, ltri_ref,
                 tm_ref, wkv_ref, s_ref, prev_ref, *, heads):
    c = pl.program_id(1)
    nseq, rows, shift_w = xb_ref.shape
    n = nseq * rows
    b_w = tm_ref.shape[-1]
    hd = b_w // heads

    @pl.when(c == 0)
    def _():
        s_ref[...] = wkv0_ref[...]
        prev_ref[...] = shift0_ref[...]

    xb = xb_ref[...].reshape(n, shift_w)
    rowi = lax.broadcasted_iota(jnp.int32, (n, 1), 0)
    prev = pltpu.roll(xb, 1, axis=0)
    for s in range(nseq):
        prev = jnp.where(rowi == s * rows, prev_ref[s], prev)
        prev_ref[s] = xb[(s + 1) * rows - 1:(s + 1) * rows, :]
    xs = xb + (prev - xb) * mu_ref[...]
    r = xs[:, 0:b_w]
    k = xs[:, b_w:2 * b_w]
    v = xs[:, 2 * b_w:3 * b_w]
    tail = xs[:, 3 * b_w:]

    dec_in = dbase_ref[...] + _mm(jnp.tanh(tail), wd_ref[...])
    w_raw = -(jnp.maximum(-dec_in, 0.0) + _softplus_tail(dec_in)) - 0.5
    logw = -jnp.exp(w_raw)
    a_lr = jax.nn.sigmoid(abase_ref[...] + _mm(tail, wa_ref[...]))
    gate = _mm(jax.nn.sigmoid(tail), wg_ref[...])
    kk = k * kk_ref[...]
    k2 = k * (1.0 + (a_lr - 1.0) * ka_ref[...])
    kk = kk / jnp.maximum(jnp.sqrt(_mm_split_l(kk * kk, gsum_ref[...])), KK_EPS)

    cum = _mm_split_r(ltri_ref[...], logw)
    e_pos = jnp.exp(cum)
    e_neg = jnp.exp(-cum)
    a_hat = -kk * jnp.exp(cum - logw)
    b_hat = kk * a_lr * e_neg
    k_hat = k2 * e_neg
    r_hat = r * e_pos

    ri = lax.broadcasted_iota(jnp.int32, (rows, rows), 0)
    ci = lax.broadcasted_iota(jnp.int32, (rows, rows), 1)
    strict = ci < ri
    incl = ci <= ri
    eye = (ci == ri).astype(F32)
    doublings = max(1, (rows - 1).bit_length())

    ch = [(s, h) for s in range(nseq) for h in range(heads)]
    cut = lambda x, s, h: x[s * rows:(s + 1) * rows, h * hd:(h + 1) * hd]
    s0 = {(s, h): s_ref[s, h] for s, h in ch}
    s0b = {i: s0[i].astype(BF16) for i in ch}
    a_b, b_b, k_b, r_b, v_b = (x.astype(BF16) for x in (a_hat, b_hat, k_hat, r_hat, v))
    ah = {i: cut(a_b, *i) for i in ch}
    bh = {i: cut(b_b, *i) for i in ch}
    kh = {i: cut(k_b, *i) for i in ch}
    rh = {i: cut(r_b, *i) for i in ch}
    vh = {i: cut(v_b, *i) for i in ch}
    m = {i: _mm(jnp.concatenate([ah[i], rh[i]], axis=0), jnp.concatenate([bh[i], kh[i]], axis=0), NT) for i in ch}
    a_ab = {i: jnp.where(strict, m[i][0:rows, 0:rows], 0.0) for i in ch}
    a_ak = {i: jnp.where(strict, m[i][0:rows, rows:], 0.0) for i in ch}
    a_rb = {i: jnp.where(incl, m[i][rows:, 0:rows], 0.0) for i in ch}
    a_rk = {i: jnp.where(incl, m[i][rows:, rows:], 0.0) for i in ch}
    u_rhs = {i: _mm(ah[i], s0b[i], NT) + _mm(a_ak[i], vh[i]) for i in ch}
    y_part = {i: _mm(rh[i], s0b[i], NT) + _mm(a_rk[i], vh[i]) for i in ch}
    s_part = {i: s0[i] + _mm(vh[i], kh[i], TN) for i in ch}
    inv = {i: eye + a_ab[i] for i in ch}
    pw = a_ab
    for _ in range(doublings - 1):
        pw = {i: _mm(pw[i], pw[i]) for i in ch}
        inv = {i: inv[i] + _mm(inv[i], pw[i]) for i in ch}
    u = {i: _mm(inv[i], u_rhs[i]) for i in ch}
    ys = {i: y_part[i] + _mm(a_rb[i], u[i]) for i in ch}
    for s, h in ch:
        g_end = e_pos[(s + 1) * rows - 1:(s + 1) * rows, h * hd:(h + 1) * hd]
        s_ref[s, h] = (s_part[s, h] + _mm(u[s, h], bh[s, h], TN)) * g_end

    y = jnp.concatenate([jnp.concatenate([ys[s, h] for h in range(heads)], axis=-1) for s in range(nseq)], axis=0)
    gmean = gsum_ref[...]
    mean = _mm_split_l(y, gmean) * (1.0 / hd)
    d = y - mean
    var = _mm_split_l(d * d, gmean) * (1.0 / hd)
    yn = d * lax.rsqrt(var + GN_EPS) * lnw_ref[...] + lnb_ref[...]
    bonus = _mm_split_l(r * k2 * rk_ref[...], gmean) * v
    tm_ref[...] = ((yn + bonus) * gate).astype(tm_ref.dtype).reshape(nseq, rows, b_w)

    @pl.when(c == pl.num_programs(1) - 1)
    def _():
        wkv_ref[...] = s_ref[...]


def _rwkv(xb2, shift0, wkv0, lp, *, seq_rows):
    n, shift_w = xb2.shape
    bsz, heads, hd, _ = wkv0.shape
    b_w = heads * hd
    tail_w = shift_w - 3 * b_w
    rows = _tile(seq_rows, 64)
    per_seq = seq_rows // rows
    nseq = _tile(bsz, 4)
    r_dec, r_a, r_gate = lp['w_decay_up'].shape[0], lp['w_a_up'].shape[0], lp['w_g_up'].shape[0]
    assert r_dec + r_a + r_gate == tail_w

    def pad_rows(w, start):
        return jnp.zeros((tail_w, b_w), BF16).at[start:start + w.shape[0]].set(w.astype(BF16))

    wd = pad_rows(lp['w_decay_up'], 0)
    wa = pad_rows(lp['w_a_up'], r_dec)
    wg = pad_rows(lp['w_g_up'], r_dec + r_a)
    ch = jnp.arange(b_w)
    gsum = (ch[:, None] // hd == ch[None, :] // hd).astype(BF16)
    pos = jnp.arange(nseq * rows)
    ltri = ((pos[None, :] <= pos[:, None]) & (pos[None, :] // rows == pos[:, None] // rows)).astype(BF16)
    vec = lambda a: a.reshape(1, -1).astype(F32)
    state_spec = pl.BlockSpec((nseq, heads, hd, hd), lambda b, c: (b, 0, 0, 0))
    vb = _const_spec((1, b_w))
    tmix, wkv_new = pl.pallas_call(
        functools.partial(_rwkv_kernel, heads=heads),
        grid=(bsz // nseq, per_seq),
        in_specs=[pl.BlockSpec((nseq, rows, shift_w), lambda b, c: (b, c, 0)),
                  pl.BlockSpec((nseq, 1, shift_w), lambda b, c: (b, 0, 0)),
                  state_spec, _const_spec((1, shift_w)),
                  _const_spec((tail_w, b_w)), _const_spec((tail_w, b_w)), _const_spec((tail_w, b_w)),
                  vb, vb, vb, vb, vb, vb, vb, _const_spec((b_w, b_w)), _const_spec((nseq * rows, nseq * rows))],
        out_specs=[pl.BlockSpec((nseq, rows, b_w), lambda b, c: (b, c, 0)), state_spec],
        out_shape=[jax.ShapeDtypeStruct((bsz, seq_rows, b_w), BF16), jax.ShapeDtypeStruct(wkv0.shape, F32)],
        scratch_shapes=[pltpu.VMEM((nseq, heads, hd, hd), F32), pltpu.VMEM((nseq, 1, shift_w), F32)],
        compiler_params=_cparams("parallel", "arbitrary"),
        name="rwkv",
    )(xb2.reshape(bsz, seq_rows, shift_w), shift0.reshape(bsz, 1, shift_w), wkv0, vec(lp['mu_shift']), wd, wa, wg,
      vec(lp['decay_base']), vec(lp['a_base']), vec(lp['k_k']), vec(lp['k_a']), vec(lp['r_k']), vec(lp['lnx_w']),
      vec(lp['lnx_b']), gsum, ltri)
    return tmix.reshape(n, b_w), wkv_new


def _ffn_kernel(x_ref, att_ref, tmix_ref, pe_ref, st0_ref, st1_ref, woa_ref, wob_ref, gffn_ref, wg_ref, wv_ref,
                cw_ref, cb_ref, wfo_ref, wpi_ref, gple_ref, wpg_ref, gfin_ref, y_ref, tail_ref,
                c0_ref, c1_ref, *, seq_rows, final_norm):
    rows = x_ref.shape[0]
    blk = min(rows, FFN_CHAIN_ROWS)
    nblk = rows // blk
    carried = seq_rows >= rows
    rowi = lax.broadcasted_iota(jnp.int32, (blk, 1), 0)
    cw = cw_ref[...]
    if carried:
        first = pl.program_id(0) % (seq_rows // rows) == 0

        @pl.when(first)
        def _():
            c0_ref[...] = st0_ref[...]
            c1_ref[...] = st1_ref[...]
    else:
        assert seq_rows & (seq_rows - 1) == 0 and blk % seq_rows == 0

    def project(c):
        rs = pl.ds(c * blk, blk)
        return (x_ref[rs, :] + jnp.dot(att_ref[rs, :], woa_ref[...], preferred_element_type=F32)
                + jnp.dot(tmix_ref[rs, :], wob_ref[...], preferred_element_type=F32))

    def up(h):
        un = _rms(h, gffn_ref[...]).astype(BF16)
        return (jnp.dot(un, wg_ref[...], preferred_element_type=F32),
                jnp.dot(un, wv_ref[...], preferred_element_type=F32))

    def conv_down(c, h, gp, val, before):
        if carried:
            tpos, (h0, h1) = rowi, before
        else:
            rs = pl.ds(c * blk, blk)
            tpos, h0, h1 = jnp.bitwise_and(rowi, seq_rows - 1), st0_ref[rs, :], st1_ref[rs, :]
        prev1 = jnp.where(tpos == 0, h1, pltpu.roll(gp, 1, axis=0))
        prev2 = jnp.where(tpos == 0, h0, jnp.where(tpos == 1, h1, pltpu.roll(gp, 2, axis=0)))
        cpre = cb_ref[...] + prev2 * cw[0:1, :] + prev1 * cw[1:2, :] + gp * cw[2:3, :]
        act = cpre * jax.nn.sigmoid(cpre) * val
        return h + _mm(act, wfo_ref[...])

    def embed_gate(c, h):
        pe_proj = _rms(_mm(pe_ref[pl.ds(c * blk, blk), :], wpi_ref[...]), gple_ref[...])
        return pe_proj, _mm(h, wpg_ref[...])

    def finish(c, h, pe_proj, gate_pre):
        h = h + jax.nn.sigmoid(gate_pre) * pe_proj
        y_ref[pl.ds(c * blk, blk), :] = _rms(h, gfin_ref[...]) if final_norm else h

    hs, ups, downs, gates = {}, {}, {}, {}
    before = (c0_ref[...], c1_ref[...]) if carried else None
    for t in range(nblk + 4):
        if t < nblk:
            hs[t] = project(t)
        c = t - 1
        if 0 <= c < nblk:
            ups[c] = up(hs[c])
        c = t - 2
        if 0 <= c < nblk:
            gp, val = ups.pop(c)
            downs[c] = conv_down(c, hs.pop(c), gp, val, before)
            before = (gp[blk - 2:blk - 1, :], gp[blk - 1:blk, :])
            if c == nblk - 1:
                if carried:
                    c0_ref[...], c1_ref[...] = before
                    tail_rows = tail_ref.shape[0]
                    tail_ref[...] = gp[blk - tail_rows:blk, :]
            if not carried:
                tail_ref[pl.ds(c * blk, blk), :] = gp
        c = t - 3
        if 0 <= c < nblk:
            gates[c] = embed_gate(c, downs[c])
        c = t - 4
        if 0 <= c < nblk:
            finish(c, downs.pop(c), *gates.pop(c))


def _ffn_tiles_within_sequence(seq_rows):
    return seq_rows % FFN_CHAIN_ROWS == 0


def _ffn(x2, att, tmix, pe2, st0, st1, lp, g_final, *, seq_rows, final_norm):
    n, d = x2.shape
    a_w = att.shape[1]
    d_ff = lp['conv_w'].shape[1]
    ple = pe2.shape[1]
    assert lp['conv_w'].shape[0] == CONV_W
    bf = lambda a: a.astype(BF16)
    vec = lambda a: a.reshape(1, -1).astype(F32)
    row = lambda i: (i, 0)
    if _ffn_tiles_within_sequence(seq_rows):
        tm = _tile(seq_rows, FFN_TILE_ROWS)
        per_seq = seq_rows // tm
        assert tm >= SUBLANES and seq_rows >= CONV_W - 1
        st_spec = pl.BlockSpec((None, 1, d_ff), lambda i: (i // per_seq, 0, 0))
        tail_rows = SUBLANES
        tail_shape = jax.ShapeDtypeStruct((n // seq_rows, tail_rows, d_ff), F32)
        tail_spec = pl.BlockSpec((None, tail_rows, d_ff), lambda i: (i // per_seq, 0, 0))
        semantics = "arbitrary"
    else:
        tm = _tile(n, FFN_TILE_ROWS)
        assert tm % seq_rows == 0
        st_spec = pl.BlockSpec((tm, d_ff), row)
        tail_rows = tm
        tail_shape = jax.ShapeDtypeStruct((n, d_ff), F32)
        tail_spec = pl.BlockSpec((tm, d_ff), row)
        semantics = "parallel"
    w_in = lp['w_ffn_in']
    return pl.pallas_call(
        functools.partial(_ffn_kernel, seq_rows=seq_rows, final_norm=final_norm),
        grid=(n // tm,),
        in_specs=[pl.BlockSpec((tm, d), row), pl.BlockSpec((tm, a_w), row), pl.BlockSpec((tm, tmix.shape[1]), row),
                  pl.BlockSpec((tm, ple), row), st_spec, st_spec,
                  _const_spec((a_w, d)), _const_spec((tmix.shape[1], d)), _const_spec((1, d)),
                  _const_spec((d, d_ff)), _const_spec((d, d_ff)), _const_spec((CONV_W, d_ff)), _const_spec((1, d_ff)),
                  _const_spec((d_ff, d)), _const_spec((ple, d)), _const_spec((1, d)), _const_spec((d, d)),
                  _const_spec((1, d))],
        out_specs=[pl.BlockSpec((tm, d), row), tail_spec],
        out_shape=[jax.ShapeDtypeStruct((n, d), F32), tail_shape],
        scratch_shapes=[pltpu.VMEM((1, d_ff), F32), pltpu.VMEM((1, d_ff), F32)],
        compiler_params=_cparams(semantics),
        name="ffn",
    )(x2, att, tmix, pe2, st0, st1, bf(lp['w_out'][:a_w]), bf(lp['w_out'][a_w:]), vec(lp['g_ffn']),
      bf(w_in[:, :d_ff]), bf(w_in[:, d_ff:]), lp['conv_w'].astype(F32), vec(lp['conv_b']), bf(lp['w_ffn_out']),
      bf(lp['w_ple_in']), vec(lp['g_ple']), bf(lp['w_ple_gate']), vec(g_final))


def _layer(x, pe, wkv0, shift0, conv0, lp, g_final, final_norm, past):
    bsz, t, d = x.shape
    heads, hd = lp['g_att'].shape
    a_w = heads * hd
    n = bsz * t
    x2 = x.reshape(n, d)
    prompt = past is None
    bias, g_att = lp['sb_bias'].astype(F32), lp['g_att'].astype(F32)
    xb2, *proj = _proj_in(x2, lp['g_mix'].reshape(1, d).astype(F32), lp['w_in'], a_w=a_w, heads=heads,
                          seq_rows=t, head_major=prompt)
    if prompt:
        qh, kt, vt, ktb, vtb = proj
        att = _sb_prompt(qh, ktb, vtb, bias * LOG2E, g_att)
        k_new, v_new = jnp.transpose(kt, (0, 3, 1, 2)), jnp.transpose(vt, (0, 3, 1, 2))
    else:
        q2, k2, v2 = proj
        cache_k, cache_v, page_table = past
        cache_kt, cache_vt = jnp.transpose(cache_k, (0, 2, 3, 1)), jnp.transpose(cache_v, (0, 2, 3, 1))
        att = _sb_sample(q2, k2, v2, cache_kt, cache_vt, page_table, bias, g_att, t_new=t)
        k_new, v_new = k2.reshape(bsz, t, heads, hd), v2.reshape(bsz, t, heads, hd)
    tmix, wkv_new = _rwkv(xb2, shift0, wkv0, lp, seq_rows=t)
    d_ff = conv0.shape[-1]
    if _ffn_tiles_within_sequence(t):
        st0, st1 = conv0[:, 0:1, :], conv0[:, 1:2, :]
    else:
        st0 = jnp.repeat(conv0[:, 0, :], t, axis=0)
        st1 = jnp.repeat(conv0[:, 1, :], t, axis=0)
    y2, tail = _ffn(x2, att, tmix, pe.reshape(n, -1), st0, st1, lp, g_final, seq_rows=t, final_norm=final_norm)
    conv_new = tail.reshape(bsz, -1, d_ff)[:, -(CONV_W - 1):, :]
    return (y2.reshape(bsz, t, d), k_new, v_new, wkv_new, xb2.reshape(bsz, t, -1)[:, -1, :], conv_new)


def kernel(x_prompt, x_sample, cache_k, cache_v, state_wkv, state_shift, state_conv, page_table, p_prompt, p_sample, g_mix, w_in, mu_shift, w_decay_up, decay_base, w_a_up, a_base, w_g_up, k_k, k_a, r_k, lnx_w, lnx_b, g_att, sb_bias, w_out, g_ffn, w_ffn_in, conv_w, conv_b, w_ffn_out, w_ple_in, g_ple, w_ple_gate, g_final):
    depth = g_mix.shape[0]
    bp = x_prompt.shape[0]
    heads_b, hd = r_k.shape[1], r_k.shape[2]
    hp, hs = x_prompt, x_sample
    outs_p, outs_s = [], []
    for i in range(depth):
        lp = dict(g_mix=g_mix[i], w_in=w_in[i], mu_shift=mu_shift[i], w_decay_up=w_decay_up[i],
                  decay_base=decay_base[i], w_a_up=w_a_up[i], a_base=a_base[i], w_g_up=w_g_up[i], k_k=k_k[i],
                  k_a=k_a[i], r_k=r_k[i], lnx_w=lnx_w[i], lnx_b=lnx_b[i], g_att=g_att[i], sb_bias=sb_bias[i],
                  w_out=w_out[i], g_ffn=g_ffn[i], w_ffn_in=w_ffn_in[i], conv_w=conv_w[i], conv_b=conv_b[i],
                  w_ffn_out=w_ffn_out[i], w_ple_in=w_ple_in[i], g_ple=g_ple[i], w_ple_gate=w_ple_gate[i])
        last = i == depth - 1
        hp, *rest_p = _layer(hp, p_prompt[i], jnp.zeros((bp, heads_b, hd, hd), F32),
                             jnp.zeros((bp, state_shift.shape[-1]), F32),
                             jnp.zeros((bp,) + state_conv.shape[2:], F32), lp, g_final, last, None)
        hs, *rest_s = _layer(hs, p_sample[i], state_wkv[i], state_shift[i], state_conv[i], lp, g_final, last,
                             (cache_k[i], cache_v[i], page_table))
        outs_p.append(rest_p)
        outs_s.append(rest_s)
    stack = lambda outs, j: jnp.stack([o[j] for o in outs])
    return (hp, hs) + tuple(stack(outs_p, j) for j in range(5)) + tuple(stack(outs_s, j) for j in range(5))
```

```python
import functools

import jax
import jax.numpy as jnp
from jax import lax
from jax.experimental import pallas as pl
from jax.experimental.pallas import tpu as pltpu

F32 = jnp.float32
BF16 = jnp.bfloat16
NORM_EPS = 1e-6
GN_EPS = 64e-5
KK_EPS = 1e-12
CONV_W = 3
V7X_VMEM_LIMIT_BYTES = 56 * 1024 * 1024
LANES = 128
SUBLANES = 8
LOG2E = 1.4426950408889634
FFN_TILE_ROWS = 256
FFN_CHAIN_ROWS = 128


def _cparams(*semantics):
    return pltpu.CompilerParams(dimension_semantics=semantics, vmem_limit_bytes=V7X_VMEM_LIMIT_BYTES)


def _const_spec(shape):
    return pl.BlockSpec(shape, lambda *_: (0,) * len(shape), pipeline_mode=pl.Buffered(1))


def _tile(n, pref):
    t = min(n, pref)
    assert n % t == 0, (n, pref)
    return t


def _rms(x, g):
    return x * lax.rsqrt(jnp.mean(x * x, axis=-1, keepdims=True) + NORM_EPS) * g


NN = (((1,), (0,)), ((), ()))
NT = (((1,), (1,)), ((), ()))
TN = (((0,), (0,)), ((), ()))


def _mm(a, b, dims=NN):
    return lax.dot_general(a.astype(BF16), b.astype(BF16), dims, preferred_element_type=F32)


def _split(x):
    hi = x.astype(BF16)
    return hi, (x - hi.astype(F32)).astype(BF16)


def _mm_split_l(x, c):
    hi, lo = _split(x)
    return jnp.dot(hi, c, preferred_element_type=F32) + jnp.dot(lo, c, preferred_element_type=F32)


def _mm_split_r(c, x):
    hi, lo = _split(x)
    return jnp.dot(c, hi, preferred_element_type=F32) + jnp.dot(c, lo, preferred_element_type=F32)


def _softplus_tail(z):
    return jnp.log(1.0 + jnp.exp(-jnp.abs(z)))


def _proj_in_kernel(x_ref, g_ref, wq_ref, wk_ref, wv_ref, wx_ref, xb_ref, *outs, heads, head_major):
    rows = x_ref.shape[0]
    a_w = wq_ref.shape[1]
    hd = a_w // heads
    u = _rms(x_ref[...], g_ref[...]).astype(BF16)
    q = jnp.dot(u, wq_ref[...], preferred_element_type=F32)
    xb_ref[...] = jnp.dot(u, wx_ref[...], preferred_element_type=F32)
    if head_major:
        qh_ref, kt_ref, vt_ref, ktb_ref, vtb_ref = outs
        kt = _mm(wk_ref[...], u, NT).reshape(heads, hd, rows)
        vt = _mm(wv_ref[...], u, NT).reshape(heads, hd, rows)
        kt_ref[...] = kt
        vt_ref[...] = vt
        ktb_ref[...] = kt.astype(BF16)
        vtb_ref[...] = vt.astype(BF16)
        qs = q * (hd ** -0.5 * LOG2E)
        for h in range(heads):
            qh_ref[h] = qs[:, h * hd:(h + 1) * hd].astype(BF16)
    else:
        q_ref, k_ref, v_ref = outs
        q_ref[...] = q
        k_ref[...] = jnp.dot(u, wk_ref[...], preferred_element_type=F32)
        v_ref[...] = jnp.dot(u, wv_ref[...], preferred_element_type=F32)


def _proj_in(x2, g, w_in, *, a_w, heads, seq_rows, head_major):
    n, d = x2.shape
    shift_w = w_in.shape[1] - 3 * a_w
    hd = a_w // heads
    tm = _tile(seq_rows if head_major else n, 512)
    row = lambda i: (i, 0)
    wq, wk, wv, wx = (w_in[:, :a_w], w_in[:, a_w:2 * a_w], w_in[:, 2 * a_w:3 * a_w], w_in[:, 3 * a_w:])
    out_shape = [jax.ShapeDtypeStruct((n, shift_w), F32)]
    out_specs = [pl.BlockSpec((tm, shift_w), row)]
    if head_major:
        bsz, per_seq = n // seq_rows, seq_rows // tm
        wk, wv = wk.T, wv.T
        kv_w_spec = _const_spec((a_w, d))
        t_spec = pl.BlockSpec((None, heads, hd, tm), lambda i: (i // per_seq, 0, 0, i % per_seq))
        out_shape += [jax.ShapeDtypeStruct((bsz, heads, seq_rows, hd), BF16)]
        out_specs += [pl.BlockSpec((None, heads, tm, hd), lambda i: (i // per_seq, 0, i % per_seq, 0))]
        out_shape += [jax.ShapeDtypeStruct((bsz, heads, hd, seq_rows), dt) for dt in (F32, F32, BF16, BF16)]
        out_specs += [t_spec] * 4
    else:
        kv_w_spec = _const_spec((d, a_w))
        out_shape += [jax.ShapeDtypeStruct((n, a_w), F32)] * 3
        out_specs += [pl.BlockSpec((tm, a_w), row)] * 3
    return pl.pallas_call(
        functools.partial(_proj_in_kernel, heads=heads, head_major=head_major),
        grid=(n // tm,),
        in_specs=[pl.BlockSpec((tm, d), row), _const_spec((1, d)), _const_spec((d, a_w)), kv_w_spec, kv_w_spec,
                  _const_spec((d, shift_w))],
        out_specs=out_specs,
        out_shape=out_shape,
        compiler_params=_cparams("parallel"),
        name="proj_in",
    )(x2, g, wq.astype(BF16), wk.astype(BF16), wv.astype(BF16), wx.astype(BF16))


def _sb_scores(z):
    log_beta = jnp.minimum(z, 0.0) - _softplus_tail(z)
    return log_beta, log_beta - z


def _sb_scores2(z2):
    log_beta = jnp.minimum(z2, 0.0) - jnp.log2(1.0 + jnp.exp2(-jnp.abs(z2)))
    return log_beta, log_beta - z2


def _sb_prompt_kernel(bias_ref, q_ref, kt_ref, vt_ref, g_ref, o_ref, acc_ref, carry_ref, *, tq, tk, heads):
    qi = pl.program_id(1)
    per_q = tq // tk
    kj = lax.broadcasted_iota(jnp.int32, (tk, tk), 0)
    ks = lax.broadcasted_iota(jnp.int32, (tk, tk), 1)
    vis = ks < kj
    later = jnp.concatenate([kj > ks, kj >= 0], axis=1).astype(BF16)

    def logits(h, rb, s0):
        return (jnp.dot(q_ref[h, pl.ds(rb * tk, tk), :], kt_ref[h, :, pl.ds(s0, tk)], preferred_element_type=F32)
                + bias_ref[h])

    def keep_sums(z, diagonal):
        log_beta, log_keep = _sb_scores2(z)
        if diagonal:
            log_keep = jnp.where(vis, log_keep, 0.0)
        return log_beta, jnp.dot(log_keep.astype(BF16), later, preferred_element_type=F32)

    def weigh(h, rb, s0, diagonal, log_beta, sums):
        rs = pl.ds(rb * tk, tk)
        a = jnp.exp2(log_beta + sums[:, :tk] + (0.0 if diagonal else carry_ref[h, rs, :]))
        if diagonal:
            a = jnp.where(vis, a, 0.0)
        pv = _mm(a, vt_ref[h, :, pl.ds(s0, tk)], NT)
        if diagonal:
            acc_ref[h, rs, :] = pv
            carry_ref[h, rs, :] = sums[:, tk:]
        else:
            acc_ref[h, rs, :] += pv
            carry_ref[h, rs, :] += sums[:, tk:]

    def visit(s0, own):
        chains = [(h, rb) for h in range(heads) for rb in range(per_q) if own is None or rb >= own]
        n = len(chains)
        z, mid = {}, {}
        for t in range(n + 2):
            if t < n:
                z[t] = logits(*chains[t], s0)
            if 0 <= t - 2 < n:
                h, rb = chains[t - 2]
                weigh(h, rb, s0, rb == own, *mid.pop(t - 2))
            if 0 <= t - 1 < n:
                mid[t - 1] = keep_sums(z.pop(t - 1), chains[t - 1][1] == own)

    for own in reversed(range(per_q)):
        visit(pl.multiple_of(qi * tq + own * tk, tk), own)

    def earlier(j, _):
        visit(pl.multiple_of((qi * per_q - 1 - j) * tk, tk), None)
        return 0

    lax.fori_loop(0, qi * per_q, earlier, 0)
    o_ref[...] = jnp.concatenate([_rms(acc_ref[h], g_ref[h:h + 1, :]) for h in range(heads)],
                                 axis=-1).astype(o_ref.dtype)


def _sb_prompt(qh, ktb, vtb, bias, g_att):
    bsz, heads, t, hd = qh.shape
    tq = _tile(t, 8 * LANES)
    tk = _tile(tq, 2 * LANES)
    kv_spec = pl.BlockSpec((None, heads, hd, t), lambda b, i: (b, 0, 0, 0))
    return pl.pallas_call(
        functools.partial(_sb_prompt_kernel, tq=tq, tk=tk, heads=heads),
        grid=(bsz, t // tq),
        in_specs=[pl.BlockSpec(memory_space=pltpu.SMEM),
                  pl.BlockSpec((None, heads, tq, hd), lambda b, i: (b, 0, i, 0)),
                  kv_spec, kv_spec, _const_spec((heads, hd))],
        out_specs=pl.BlockSpec((tq, heads * hd), lambda b, i: (b * (t // tq) + i, 0)),
        out_shape=jax.ShapeDtypeStruct((bsz * t, heads * hd), BF16),
        scratch_shapes=[pltpu.VMEM((heads, tq, hd), F32), pltpu.VMEM((heads, tq, tk), F32)],
        compiler_params=_cparams("parallel", "arbitrary"),
        name="sb_prompt",
    )(bias, qh, ktb, vtb, g_att)


def _sb_sample_kernel(pt_ref, bias_ref, q_ref, kn_ref, vn_ref, g_ref, *rest, heads, pp):
    del pt_ref
    k_refs, v_refs = rest[:pp], rest[pp:2 * pp]
    o_ref = rest[2 * pp]
    qbd_ref, bcol_ref, acc_ref, carry_ref = rest[2 * pp + 1:]
    p = pl.program_id(1)
    t_new, a_w = q_ref.shape
    hd = a_w // heads
    page = k_refs[0].shape[-1]
    lane = lax.broadcasted_iota(jnp.int32, (t_new, a_w), 1)
    own = jnp.concatenate([(lane >= h * hd) & (lane < (h + 1) * hd) for h in range(heads)], axis=0)

    def later_mask(m):
        return (lax.broadcasted_iota(jnp.int32, (m, m), 0) > lax.broadcasted_iota(jnp.int32, (m, m), 1)).astype(BF16)

    @pl.when(p == 0)
    def _():
        q = q_ref[...] * (hd ** -0.5)
        qbd_ref[...] = jnp.where(own, jnp.concatenate([q] * heads, axis=0), 0.0).astype(BF16)
        bcol_ref[...] = jnp.concatenate([jnp.full((t_new, 1), bias_ref[h], F32) for h in range(heads)], axis=0)
        z = _mm(qbd_ref[...], kn_ref[...], NT) + bcol_ref[...]
        vis1 = (lax.broadcasted_iota(jnp.int32, (t_new, t_new), 1)
                < lax.broadcasted_iota(jnp.int32, (t_new, t_new), 0))
        vis = jnp.concatenate([vis1] * heads, axis=0)
        log_beta, log_keep = _sb_scores(z)
        log_keep = jnp.where(vis, log_keep, 0.0)
        rest_ = jnp.dot(log_keep.astype(BF16), later_mask(t_new), preferred_element_type=F32)
        a = jnp.where(vis, jnp.exp(log_beta + rest_), 0.0)
        acc_ref[...] = _mm(a, vn_ref[...])
        carry_ref[...] = jnp.broadcast_to(jnp.sum(log_keep, axis=-1, keepdims=True), carry_ref.shape)

    ones = lax.broadcasted_iota(jnp.int32, (page, page), 0) >= 0
    later = jnp.concatenate([later_mask(page), ones.astype(BF16)], axis=1)
    js = range(pp)
    qbd, bcol = qbd_ref[...], bcol_ref[...]
    z = [_mm(qbd, k_refs[j][...].reshape(a_w, page)) + bcol for j in js]
    scores = [_sb_scores(z[j]) for j in js]
    sums = [jnp.dot(scores[j][1].astype(BF16), later, preferred_element_type=F32) for j in js]
    carry = carry_ref[...]
    a = []
    for j in js:
        a.append(jnp.exp(scores[j][0] + sums[j][:, :page] + carry))
        carry = carry + sums[j][:, page:]
    carry_ref[...] = carry
    acc = acc_ref[...]
    for j in js:
        acc = acc + _mm(a[j], v_refs[j][...].reshape(a_w, page), NT)
    acc_ref[...] = acc

    @pl.when(p == pl.num_programs(1) - 1)
    def _():
        att = jnp.where(own, acc, 0.0).reshape(heads, t_new, a_w).sum(axis=0)
        o_ref[...] = jnp.concatenate([_rms(att[:, h * hd:(h + 1) * hd], g_ref[h:h + 1, :]) for h in range(heads)],
                                     axis=-1).astype(o_ref.dtype)


def _sb_sample(q2, k2, v2, cache_kt, cache_vt, page_table, bias, g_att, *, t_new):
    n, a_w = q2.shape
    heads, hd = g_att.shape
    bsz = n // t_new
    page = cache_kt.shape[-1]
    n_pages = page_table.shape[1]
    pp = _tile(n_pages, 16)

    def page_spec(j):
        return pl.BlockSpec((None, heads, hd, page), lambda b, p, pt: (pt[b, n_pages - 1 - p * pp - j], 0, 0, 0))

    tok = pl.BlockSpec((t_new, a_w), lambda b, p, pt: (b, 0))
    grid_spec = pltpu.PrefetchScalarGridSpec(
        num_scalar_prefetch=1,
        grid=(bsz, n_pages // pp),
        in_specs=[pl.BlockSpec(memory_space=pltpu.SMEM), tok, tok, tok,
                  pl.BlockSpec((heads, hd), lambda b, p, pt: (0, 0))] + [page_spec(j) for j in range(pp)] * 2,
        out_specs=tok,
        scratch_shapes=[pltpu.VMEM((heads * t_new, a_w), BF16), pltpu.VMEM((heads * t_new, 1), F32),
                        pltpu.VMEM((heads * t_new, a_w), F32), pltpu.VMEM((heads * t_new, page), F32)],
    )
    return pl.pallas_call(
        functools.partial(_sb_sample_kernel, heads=heads, pp=pp),
        grid_spec=grid_spec,
        out_shape=jax.ShapeDtypeStruct((n, a_w), BF16),
        compiler_params=_cparams("parallel", "arbitrary"),
        name="sb_sample",
    )(page_table, bias, q2, k2, v2, g_att, *([cache_kt] * pp), *([cache_vt] * pp))


def _rwkv_kernel(xb_ref, shift0_ref, wkv0_ref, mu_ref, wd_ref, wa_ref, wg_ref, dbase_ref, abase_ref,
                 kk_ref, ka_ref, rk_ref, lnw_ref, lnb_ref, gsum_ref, ltri_ref,
                 tm_ref, wkv_ref, s_ref, prev_ref, *, heads):
    c = pl.program_id(1)
    nseq, rows, shift_w = xb_ref.shape
    n = nseq * rows
    b_w = tm_ref.shape[-1]
    hd = b_w // heads

    @pl.when(c == 0)
    def _():
        s_ref[...] = wkv0_ref[...]
        prev_ref[...] = shift0_ref[...]

    xb = xb_ref[...].reshape(n, shift_w)
    rowi = lax.broadcasted_iota(jnp.int32, (n, 1), 0)
    prev = pltpu.roll(xb, 1, axis=0)
    for s in range(nseq):
        prev = jnp.where(rowi == s * rows, prev_ref[s], prev)
        prev_ref[s] = xb[(s + 1) * rows - 1:(s + 1) * rows, :]
    xs = xb + (prev - xb) * mu_ref[...]
    r = xs[:, 0:b_w]
    k = xs[:, b_w:2 * b_w]
    v = xs[:, 2 * b_w:3 * b_w]
    tail = xs[:, 3 * b_w:]

    dec_in = dbase_ref[...] + _mm(jnp.tanh(tail), wd_ref[...])
    w_raw = -(jnp.maximum(-dec_in, 0.0) + _softplus_tail(dec_in)) - 0.5
    logw = -jnp.exp(w_raw)
    a_lr = jax.nn.sigmoid(abase_ref[...] + _mm(tail, wa_ref[...]))
    gate = _mm(jax.nn.sigmoid(tail), wg_ref[...])
    kk = k * kk_ref[...]
    k2 = k * (1.0 + (a_lr - 1.0) * ka_ref[...])
    kk = kk / jnp.maximum(jnp.sqrt(_mm_split_l(kk * kk, gsum_ref[...])), KK_EPS)

    cum = _mm_split_r(ltri_ref[...], logw)
    e_pos = jnp.exp(cum)
    e_neg = jnp.exp(-cum)
    a_hat = -kk * jnp.exp(cum - logw)
    b_hat = kk * a_lr * e_neg
    k_hat = k2 * e_neg
    r_hat = r * e_pos

    ri = lax.broadcasted_iota(jnp.int32, (rows, rows), 0)
    ci = lax.broadcasted_iota(jnp.int32, (rows, rows), 1)
    strict = ci < ri
    incl = ci <= ri
    eye = (ci == ri).astype(F32)
    doublings = max(1, (rows - 1).bit_length())

    ch = [(s, h) for s in range(nseq) for h in range(heads)]
    cut = lambda x, s, h: x[s * rows:(s + 1) * rows, h * hd:(h + 1) * hd]
    s0 = {(s, h): s_ref[s, h] for s, h in ch}
    s0b = {i: s0[i].astype(BF16) for i in ch}
    a_b, b_b, k_b, r_b, v_b = (x.astype(BF16) for x in (a_hat, b_hat, k_hat, r_hat, v))
    ah = {i: cut(a_b, *i) for i in ch}
    bh = {i: cut(b_b, *i) for i in ch}
    kh = {i: cut(k_b, *i) for i in ch}
    rh = {i: cut(r_b, *i) for i in ch}
    vh = {i: cut(v_b, *i) for i in ch}
    m = {i: _mm(jnp.concatenate([ah[i], rh[i]], axis=0), jnp.concatenate([bh[i], kh[i]], axis=0), NT) for i in ch}
    a_ab = {i: jnp.where(strict, m[i][0:rows, 0:rows], 0.0) for i in ch}
    a_ak = {i: jnp.where(strict, m[i][0:rows, rows:], 0.0) for i in ch}
    a_rb = {i: jnp.where(incl, m[i][rows:, 0:rows], 0.0) for i in ch}
    a_rk = {i: jnp.where(incl, m[i][rows:, rows:], 0.0) for i in ch}
    u_rhs = {i: _mm(ah[i], s0b[i], NT) + _mm(a_ak[i], vh[i]) for i in ch}
    y_part = {i: _mm(rh[i], s0b[i], NT) + _mm(a_rk[i], vh[i]) for i in ch}
    s_part = {i: s0[i] + _mm(vh[i], kh[i], TN) for i in ch}
    inv = {i: eye + a_ab[i] for i in ch}
    pw = a_ab
    for _ in range(doublings - 1):
        pw = {i: _mm(pw[i], pw[i]) for i in ch}
        inv = {i: inv[i] + _mm(inv[i], pw[i]) for i in ch}
    u = {i: _mm(inv[i], u_rhs[i]) for i in ch}
    ys = {i: y_part[i] + _mm(a_rb[i], u[i]) for i in ch}
    for s, h in ch:
        g_end = e_pos[(s + 1) * rows - 1:(s + 1) * rows, h * hd:(h + 1) * hd]
        s_ref[s, h] = (s_part[s, h] + _mm(u[s, h], bh[s, h], TN)) * g_end

    y = jnp.concatenate([jnp.concatenate([ys[s, h] for h in range(heads)], axis=-1) for s in range(nseq)], axis=0)
    gmean = gsum_ref[...]
    mean = _mm_split_l(y, gmean) * (1.0 / hd)
    d = y - mean
    var = _mm_split_l(d * d, gmean) * (1.0 / hd)
    yn = d * lax.rsqrt(var + GN_EPS) * lnw_ref[...] + lnb_ref[...]
    bonus = _mm_split_l(r * k2 * rk_ref[...], gmean) * v
    tm_ref[...] = ((yn + bonus) * gate).astype(tm_ref.dtype).reshape(nseq, rows, b_w)

    @pl.when(c == pl.num_programs(1) - 1)
    def _():
        wkv_ref[...] = s_ref[...]


def _rwkv(xb2, shift0, wkv0, lp, *, seq_rows):
    n, shift_w = xb2.shape
    bsz, heads, hd, _ = wkv0.shape
    b_w = heads * hd
    tail_w = shift_w - 3 * b_w
    rows = _tile(seq_rows, 64)
    per_seq = seq_rows // rows
    nseq = _tile(bsz, 4)
    r_dec, r_a, r_gate = lp['w_decay_up'].shape[0], lp['w_a_up'].shape[0], lp['w_g_up'].shape[0]
    assert r_dec + r_a + r_gate == tail_w

    def pad_rows(w, start):
        return jnp.zeros((tail_w, b_w), BF16).at[start:start + w.shape[0]].set(w.astype(BF16))

    wd = pad_rows(lp['w_decay_up'], 0)
    wa = pad_rows(lp['w_a_up'], r_dec)
    wg = pad_rows(lp['w_g_up'], r_dec + r_a)
    ch = jnp.arange(b_w)
    gsum = (ch[:, None] // hd == ch[None, :] // hd).astype(BF16)
    pos = jnp.arange(nseq * rows)
    ltri = ((pos[None, :] <= pos[:, None]) & (pos[None, :] // rows == pos[:, None] // rows)).astype(BF16)
    vec = lambda a: a.reshape(1, -1).astype(F32)
    state_spec = pl.BlockSpec((nseq, heads, hd, hd), lambda b, c: (b, 0, 0, 0))
    vb = _const_spec((1, b_w))
    tmix, wkv_new = pl.pallas_call(
        functools.partial(_rwkv_kernel, heads=heads),
        grid=(bsz // nseq, per_seq),
        in_specs=[pl.BlockSpec((nseq, rows, shift_w), lambda b, c: (b, c, 0)),
                  pl.BlockSpec((nseq, 1, shift_w), lambda b, c: (b, 0, 0)),
                  state_spec, _const_spec((1, shift_w)),
                  _const_spec((tail_w, b_w)), _const_spec((tail_w, b_w)), _const_spec((tail_w, b_w)),
                  vb, vb, vb, vb, vb, vb, vb, _const_spec((b_w, b_w)), _const_spec((nseq * rows, nseq * rows))],
        out_specs=[pl.BlockSpec((nseq, rows, b_w), lambda b, c: (b, c, 0)), state_spec],
        out_shape=[jax.ShapeDtypeStruct((bsz, seq_rows, b_w), BF16), jax.ShapeDtypeStruct(wkv0.shape, F32)],
        scratch_shapes=[pltpu.VMEM((nseq, heads, hd, hd), F32), pltpu.VMEM((nseq, 1, shift_w), F32)],
        compiler_params=_cparams("parallel", "arbitrary"),
        name="rwkv",
    )(xb2.reshape(bsz, seq_rows, shift_w), shift0.reshape(bsz, 1, shift_w), wkv0, vec(lp['mu_shift']), wd, wa, wg,
      vec(lp['decay_base']), vec(lp['a_base']), vec(lp['k_k']), vec(lp['k_a']), vec(lp['r_k']), vec(lp['lnx_w']),
      vec(lp['lnx_b']), gsum, ltri)
    return tmix.reshape(n, b_w), wkv_new


def _ffn_kernel(x_ref, att_ref, tmix_ref, pe_ref, st0_ref, st1_ref, woa_ref, wob_ref, gffn_ref, wg_ref, wv_ref,
                cw_ref, cb_ref, wfo_ref, wpi_ref, gple_ref, wpg_ref, gfin_ref, y_ref, tail_ref,
                c0_ref, c1_ref, *, seq_rows, final_norm):
    rows = x_ref.shape[0]
    blk = min(rows, FFN_CHAIN_ROWS)
    nblk = rows // blk
    carried = seq_rows >= rows
    rowi = lax.broadcasted_iota(jnp.int32, (blk, 1), 0)
    cw = cw_ref[...]
    if carried:
        first = pl.program_id(0) % (seq_rows // rows) == 0

        @pl.when(first)
        def _():
            c0_ref[...] = st0_ref[...]
            c1_ref[...] = st1_ref[...]
    else:
        assert seq_rows & (seq_rows - 1) == 0 and blk % seq_rows == 0

    def project(c):
        rs = pl.ds(c * blk, blk)
        return (x_ref[rs, :] + jnp.dot(att_ref[rs, :], woa_ref[...], preferred_element_type=F32)
                + jnp.dot(tmix_ref[rs, :], wob_ref[...], preferred_element_type=F32))

    def up(h):
        un = _rms(h, gffn_ref[...]).astype(BF16)
        return (jnp.dot(un, wg_ref[...], preferred_element_type=F32),
                jnp.dot(un, wv_ref[...], preferred_element_type=F32))

    def conv_down(c, h, gp, val, before):
        if carried:
            tpos, (h0, h1) = rowi, before
        else:
            rs = pl.ds(c * blk, blk)
            tpos, h0, h1 = jnp.bitwise_and(rowi, seq_rows - 1), st0_ref[rs, :], st1_ref[rs, :]
        prev1 = jnp.where(tpos == 0, h1, pltpu.roll(gp, 1, axis=0))
        prev2 = jnp.where(tpos == 0, h0, jnp.where(tpos == 1, h1, pltpu.roll(gp, 2, axis=0)))
        cpre = cb_ref[...] + prev2 * cw[0:1, :] + prev1 * cw[1:2, :] + gp * cw[2:3, :]
        act = cpre * jax.nn.sigmoid(cpre) * val
        return h + _mm(act, wfo_ref[...])

    def embed_gate(c, h):
        pe_proj = _rms(_mm(pe_ref[pl.ds(c * blk, blk), :], wpi_ref[...]), gple_ref[...])
        return pe_proj, _mm(h, wpg_ref[...])

    def finish(c, h, pe_proj, gate_pre):
        h = h + jax.nn.sigmoid(gate_pre) * pe_proj
        y_ref[pl.ds(c * blk, blk), :] = _rms(h, gfin_ref[...]) if final_norm else h

    hs, ups, downs, gates = {}, {}, {}, {}
    before = (c0_ref[...], c1_ref[...]) if carried else None
    for t in range(nblk + 4):
        if t < nblk:
            hs[t] = project(t)
        c = t - 1
        if 0 <= c < nblk:
            ups[c] = up(hs[c])
        c = t - 2
        if 0 <= c < nblk:
            gp, val = ups.pop(c)
            downs[c] = conv_down(c, hs.pop(c), gp, val, before)
            before = (gp[blk - 2:blk - 1, :], gp[blk - 1:blk, :])
            if c == nblk - 1:
                if carried:
                    c0_ref[...], c1_ref[...] = before
                    tail_rows = tail_ref.shape[0]
                    tail_ref[...] = gp[blk - tail_rows:blk, :]
            if not carried:
                tail_ref[pl.ds(c * blk, blk), :] = gp
        c = t - 3
        if 0 <= c < nblk:
            gates[c] = embed_gate(c, downs[c])
        c = t - 4
        if 0 <= c < nblk:
            finish(c, downs.pop(c), *gates.pop(c))


def _ffn_tiles_within_sequence(seq_rows):
    return seq_rows % FFN_CHAIN_ROWS == 0


def _ffn(x2, att, tmix, pe2, st0, st1, lp, g_final, *, seq_rows, final_norm):
    n, d = x2.shape
    a_w = att.shape[1]
    d_ff = lp['conv_w'].shape[1]
    ple = pe2.shape[1]
    assert lp['conv_w'].shape[0] == CONV_W
    bf = lambda a: a.astype(BF16)
    vec = lambda a: a.reshape(1, -1).astype(F32)
    row = lambda i: (i, 0)
    if _ffn_tiles_within_sequence(seq_rows):
        tm = _tile(seq_rows, FFN_TILE_ROWS)
        per_seq = seq_rows // tm
        assert tm >= SUBLANES and seq_rows >= CONV_W - 1
        st_spec = pl.BlockSpec((None, 1, d_ff), lambda i: (i // per_seq, 0, 0))
        tail_rows = SUBLANES
        tail_shape = jax.ShapeDtypeStruct((n // seq_rows, tail_rows, d_ff), F32)
        tail_spec = pl.BlockSpec((None, tail_rows, d_ff), lambda i: (i // per_seq, 0, 0))
        semantics = "arbitrary"
    else:
        tm = _tile(n, FFN_TILE_ROWS)
        assert tm % seq_rows == 0
        st_spec = pl.BlockSpec((tm, d_ff), row)
        tail_rows = tm
        tail_shape = jax.ShapeDtypeStruct((n, d_ff), F32)
        tail_spec = pl.BlockSpec((tm, d_ff), row)
        semantics = "parallel"
    w_in = lp['w_ffn_in']
    return pl.pallas_call(
        functools.partial(_ffn_kernel, seq_rows=seq_rows, final_norm=final_norm),
        grid=(n // tm,),
        in_specs=[pl.BlockSpec((tm, d), row), pl.BlockSpec((tm, a_w), row), pl.BlockSpec((tm, tmix.shape[1]), row),
                  pl.BlockSpec((tm, ple), row), st_spec, st_spec,
                  _const_spec((a_w, d)), _const_spec((tmix.shape[1], d)), _const_spec((1, d)),
                  _const_spec((d, d_ff)), _const_spec((d, d_ff)), _const_spec((CONV_W, d_ff)), _const_spec((1, d_ff)),
                  _const_spec((d_ff, d)), _const_spec((ple, d)), _const_spec((1, d)), _const_spec((d, d)),
                  _const_spec((1, d))],
        out_specs=[pl.BlockSpec((tm, d), row), tail_spec],
        out_shape=[jax.ShapeDtypeStruct((n, d), F32), tail_shape],
        scratch_shapes=[pltpu.VMEM((1, d_ff), F32), pltpu.VMEM((1, d_ff), F32)],
        compiler_params=_cparams(semantics),
        name="ffn",
    )(x2, att, tmix, pe2, st0, st1, bf(lp['w_out'][:a_w]), bf(lp['w_out'][a_w:]), vec(lp['g_ffn']),
      bf(w_in[:, :d_ff]), bf(w_in[:, d_ff:]), lp['conv_w'].astype(F32), vec(lp['conv_b']), bf(lp['w_ffn_out']),
      bf(lp['w_ple_in']), vec(lp['g_ple']), bf(lp['w_ple_gate']), vec(g_final))


def _layer(x, pe, wkv0, shift0, conv0, lp, g_final, final_norm, past):
    bsz, t, d = x.shape
    heads, hd = lp['g_att'].shape
    a_w = heads * hd
    n = bsz * t
    x2 = x.reshape(n, d)
    prompt = past is None
    bias, g_att = lp['sb_bias'].astype(F32), lp['g_att'].astype(F32)
    xb2, *proj = _proj_in(x2, lp['g_mix'].reshape(1, d).astype(F32), lp['w_in'], a_w=a_w, heads=heads,
                          seq_rows=t, head_major=prompt)
    if prompt:
        qh, kt, vt, ktb, vtb = proj
        att = _sb_prompt(qh, ktb, vtb, bias * LOG2E, g_att)
        k_new, v_new = jnp.transpose(kt, (0, 3, 1, 2)), jnp.transpose(vt, (0, 3, 1, 2))
    else:
        q2, k2, v2 = proj
        cache_k, cache_v, page_table = past
        cache_kt, cache_vt = jnp.transpose(cache_k, (0, 2, 3, 1)), jnp.transpose(cache_v, (0, 2, 3, 1))
        att = _sb_sample(q2, k2, v2, cache_kt, cache_vt, page_table, bias, g_att, t_new=t)
        k_new, v_new = k2.reshape(bsz, t, heads, hd), v2.reshape(bsz, t, heads, hd)
    tmix, wkv_new = _rwkv(xb2, shift0, wkv0, lp, seq_rows=t)
    d_ff = conv0.shape[-1]
    if _ffn_tiles_within_sequence(t):
        st0, st1 = conv0[:, 0:1, :], conv0[:, 1:2, :]
    else:
        st0 = jnp.repeat(conv0[:, 0, :], t, axis=0)
        st1 = jnp.repeat(conv0[:, 1, :], t, axis=0)
    y2, tail = _ffn(x2, att, tmix, pe.reshape(n, -1), st0, st1, lp, g_final, seq_rows=t, final_norm=final_norm)
    conv_new = tail.reshape(bsz, -1, d_ff)[:, -(CONV_W - 1):, :]
    return (y2.reshape(bsz, t, d), k_new, v_new, wkv_new, xb2.reshape(bsz, t, -1)[:, -1, :], conv_new)


def kernel(x_prompt, x_sample, cache_k, cache_v, state_wkv, state_shift, state_conv, page_table, p_prompt, p_sample, g_mix, w_in, mu_shift, w_decay_up, decay_base, w_a_up, a_base, w_g_up, k_k, k_a, r_k, lnx_w, lnx_b, g_att, sb_bias, w_out, g_ffn, w_ffn_in, conv_w, conv_b, w_ffn_out, w_ple_in, g_ple, w_ple_gate, g_final):
    depth = g_mix.shape[0]
    bp = x_prompt.shape[0]
    heads_b, hd = r_k.shape[1], r_k.shape[2]
    hp, hs = x_prompt, x_sample
    outs_p, outs_s = [], []
    for i in range(depth):
        lp = dict(g_mix=g_mix[i], w_in=w_in[i], mu_shift=mu_shift[i], w_decay_up=w_decay_up[i],
                  decay_base=decay_base[i], w_a_up=w_a_up[i], a_base=a_base[i], w_g_up=w_g_up[i], k_k=k_k[i],
                  k_a=k_a[i], r_k=r_k[i], lnx_w=lnx_w[i], lnx_b=lnx_b[i], g_att=g_att[i], sb_bias=sb_bias[i],
                  w_out=w_out[i], g_ffn=g_ffn[i], w_ffn_in=w_ffn_in[i], conv_w=conv_w[i], conv_b=conv_b[i],
                  w_ffn_out=w_ffn_out[i], w_ple_in=w_ple_in[i], g_ple=g_ple[i], w_ple_gate=w_ple_gate[i])
        last = i == depth - 1
        hp, *rest_p = _layer(hp, p_prompt[i], jnp.zeros((bp, heads_b, hd, hd), F32),
                             jnp.zeros((bp, state_shift.shape[-1]), F32),
                             jnp.zeros((bp,) + state_conv.shape[2:], F32), lp, g_final, last, None)
        hs, *rest_s = _layer(hs, p_sample[i], state_wkv[i], state_shift[i], state_conv[i], lp, g_final, last,
                             (cache_k[i], cache_v[i], page_table))
        outs_p.append(rest_p)
        outs_s.append(rest_s)
    stack = lambda outs, j: jnp.stack([o[j] for o in outs])
    return (hp, hs) + tuple(stack(outs_p, j) for j in range(5)) + tuple(stack(outs_s, j) for j in range(5))
```
